```python
import jax
import jax.numpy as jnp
from jax import lax
import numpy as np

D_MODEL = 2048
BATCH = 2
SEQ = 16384
DEPTH = 4

GRID_W = 64
CTX_LEN = 256
N_MIXERS = 3
N_POOL_LAYERS = (DEPTH + 2) // 3
N_NA_LAYERS = (DEPTH + 1) // 3
N_CONV_LAYERS = DEPTH // 3
POOL_WINDOWS = (2, 4, 8, 16)
N_POOL_GROUPS = 4
POOL_GROUP = D_MODEL // N_POOL_GROUPS
NA_HEAD_DIM = 64
NA_HEADS = D_MODEL // NA_HEAD_DIM
WIN_H = 8
WIN_W = 16
CONV_WIDTH = 31
FFN_CONV_WIDTH = 3
D_FF = ((8 * D_MODEL // 3 + 255) // 256) * 256
N_MOD = 6
NORM_EPS = 1e-6
LN_EPS = 1e-5
NEG_INF = -1e30

kernel_name = 'hybrid_pool_natten_conformer_dit'


def _rms_norm(x):
    xf = x.astype(jnp.float32)
    y = xf * lax.rsqrt(jnp.mean(xf * xf, axis=-1, keepdims=True) + NORM_EPS)
    return y.astype(x.dtype)


def _layer_norm(x, g, b):
    xf = x.astype(jnp.float32)
    mu = jnp.mean(xf, axis=-1, keepdims=True)
    var = jnp.mean(jnp.square(xf - mu), axis=-1, keepdims=True)
    return ((xf - mu) * lax.rsqrt(var + LN_EPS)).astype(x.dtype) * g + b


def _modulation(cvec, w, b):
    m = jax.nn.silu(cvec) @ w + b
    return m.reshape(cvec.shape[0], N_MOD, 1, cvec.shape[-1])


def _ada_in(x, m, k):
    return _rms_norm(x) * (1 + m[:, k + 1]) + m[:, k]


def _depthwise_conv(u, w, b):
    k = w.shape[0]
    y = lax.conv_general_dilated(u, w[:, None, :], window_strides=(1,),
                                 padding=[(k // 2, k - 1 - k // 2)],
                                 dimension_numbers=('NWC', 'WIO', 'NWC'),
                                 feature_group_count=u.shape[-1])
    return y + b


def pool_mixer(h, w_grp, ls):
    B, L, D = h.shape
    hf = h.astype(jnp.float32)
    csum = jnp.concatenate([jnp.zeros((B, 1, D), jnp.float32), jnp.cumsum(hf, axis=1)], axis=1)
    t = jnp.arange(L)
    pooled = []
    for g, w in enumerate(POOL_WINDOWS):
        lo = jnp.clip(t - w // 2, 0, L)
        hi = jnp.clip(t - w // 2 + w, 0, L)
        cg = csum[..., g * POOL_GROUP:(g + 1) * POOL_GROUP]
        s = jnp.take(cg, hi, axis=1) - jnp.take(cg, lo, axis=1)
        pooled.append(s / (hi - lo).astype(jnp.float32)[None, :, None])
    pooled = jnp.stack(pooled, axis=2)
    d = (pooled - hf.reshape(B, L, N_POOL_GROUPS, POOL_GROUP)).astype(h.dtype)
    y = jnp.einsum('blgc,gce->blge', d, w_grp).reshape(B, L, D)
    return y * ls


def neighbourhood_attention(q, k, v, kc, vc, rpb):
    B, N, H, Dh = q.shape
    rows = N // GRID_W
    win_h = min(WIN_H, rows)
    scale = Dh ** -0.5
    qg = q.reshape(B, rows, GRID_W, H, Dh)
    kg = k.reshape(B, rows, GRID_W, H, Dh)
    vg = v.reshape(B, rows, GRID_W, H, Dh)
    col = jnp.arange(GRID_W)
    col_start = jnp.clip(col - WIN_W // 2, 0, GRID_W - WIN_W)
    col_valid = (col[None, :] >= col_start[:, None]) & (col[None, :] < col_start[:, None] + WIN_W)
    col_off = jnp.clip(col[None, :] - col[:, None], -(WIN_W - 1), WIN_W - 1) + (WIN_W - 1)
    n_loc = win_h * GRID_W

    def row_block(r):
        r0 = jnp.clip(r - win_h // 2, 0, rows - win_h)
        qr = lax.dynamic_index_in_dim(qg, r, axis=1, keepdims=False)
        kr = lax.dynamic_slice_in_dim(kg, r0, win_h, axis=1)
        vr = lax.dynamic_slice_in_dim(vg, r0, win_h, axis=1)
        s_loc = jnp.einsum('bqhd,bkjhd->bhqkj', qr, kr, preferred_element_type=jnp.float32) * scale
        row_off = r0 + jnp.arange(win_h) - r + (WIN_H - 1)
        bias = rpb[:, row_off[:, None, None], col_off[None, :, :]]
        bias = bias.transpose(0, 2, 1, 3).astype(jnp.float32)
        s_loc = jnp.where(col_valid[None, None, :, None, :], s_loc + bias[None], NEG_INF)
        s_ctx = jnp.einsum('bqhd,bmhd->bhqm', qr, kc, preferred_element_type=jnp.float32) * scale
        s = jnp.concatenate([s_loc.reshape(B, H, GRID_W, n_loc), s_ctx], axis=-1)
        p = jax.nn.softmax(s, axis=-1).astype(v.dtype)
        p_loc = p[..., :n_loc].reshape(B, H, GRID_W, win_h, GRID_W)
        p_ctx = p[..., n_loc:]
        return (jnp.einsum('bhqkj,bkjhd->bqhd', p_loc, vr)
                + jnp.einsum('bhqm,bmhd->bqhd', p_ctx, vc))

    out = lax.map(row_block, jnp.arange(rows))
    return out.transpose(1, 0, 2, 3, 4).reshape(B, N, H * Dh)


def context_attention(qc, kc, vc):
    s = jnp.einsum('blhd,bmhd->bhlm', qc, kc, preferred_element_type=jnp.float32) * (qc.shape[-1] ** -0.5)
    p = jax.nn.softmax(s, axis=-1).astype(vc.dtype)
    o = jnp.einsum('bhlm,bmhd->blhd', p, vc)
    return o.reshape(qc.shape[0], qc.shape[1], -1)


def na_mixer(h, hc, w_qkv, w_o, rpb, with_ctx):
    B, N, D = h.shape
    Lc = hc.shape[1]
    qkv = (h @ w_qkv).reshape(B, N, 3, NA_HEADS, NA_HEAD_DIM)
    kvc = (hc @ w_qkv[:, D:]).reshape(B, Lc, 2, NA_HEADS, NA_HEAD_DIM)
    kc, vc = kvc[:, :, 0], kvc[:, :, 1]
    y = neighbourhood_attention(qkv[:, :, 0], qkv[:, :, 1], qkv[:, :, 2], kc, vc, rpb) @ w_o
    yc = None
    if with_ctx:
        qc = (hc @ w_qkv[:, :D]).reshape(B, Lc, NA_HEADS, NA_HEAD_DIM)
        yc = context_attention(qc, kc, vc) @ w_o
    return y, yc


def conv_module(h, w_pw1, b_pw1, w_dw, b_dw, ln_g, ln_b, w_pw2, b_pw2):
    a, g = jnp.split(h @ w_pw1 + b_pw1, 2, axis=-1)
    u = a * jax.nn.sigmoid(g)
    u = _depthwise_conv(u, w_dw, b_dw)
    u = jax.nn.silu(_layer_norm(u, ln_g, ln_b))
    return u @ w_pw2 + b_pw2


def conv_ffn(h, w_up, w_dw, b_dw, w_down):
    u = _depthwise_conv(h @ w_up, w_dw, b_dw)
    g, v = jnp.split(u, 2, axis=-1)
    return (jax.nn.silu(g) * v) @ w_down


def setup_inputs(seed: int = 0) -> dict:
    key = jax.random.key(seed)
    ks = jax.random.split(key, 24)

    def nrm(k, shape, s):
        return jax.random.normal(k, shape, jnp.float32) * s

    D, F2 = D_MODEL, 2 * D_FF
    return {
        'x': nrm(ks[0], (BATCH, SEQ, D), 1.0),
        'c': nrm(ks[1], (BATCH, D), 1.0),
        'ctx': nrm(ks[2], (BATCH, CTX_LEN, D), 1.0),
        'c_ctx': nrm(ks[3], (D,), 1.0),
        'w_mod': nrm(ks[4], (DEPTH, D, N_MOD * D), 0.5 * D ** -0.5),
        'b_mod': nrm(ks[5], (DEPTH, N_MOD * D), 0.02),
        'pool_w': nrm(ks[6], (N_POOL_LAYERS, N_POOL_GROUPS, POOL_GROUP, POOL_GROUP), POOL_GROUP ** -0.5),
        'pool_scale': 1.0 + nrm(ks[7], (N_POOL_LAYERS, D), 0.1),
        'na_w_qkv': nrm(ks[8], (N_NA_LAYERS, D, 3 * D), D ** -0.5),
        'na_w_o': nrm(ks[9], (N_NA_LAYERS, D, D), D ** -0.5),
        'na_rpb': nrm(ks[10], (N_NA_LAYERS, NA_HEADS, 2 * WIN_H - 1, 2 * WIN_W - 1), 0.5),
        'cv_w_pw1': nrm(ks[11], (N_CONV_LAYERS, D, 2 * D), D ** -0.5),
        'cv_b_pw1': nrm(ks[12], (N_CONV_LAYERS, 2 * D), 0.02),
        'cv_w_dw': nrm(ks[13], (N_CONV_LAYERS, CONV_WIDTH, D), CONV_WIDTH ** -0.5),
        'cv_b_dw': nrm(ks[14], (N_CONV_LAYERS, D), 0.02),
        'cv_ln_g': 1.0 + nrm(ks[15], (N_CONV_LAYERS, D), 0.1),
        'cv_ln_b': nrm(ks[16], (N_CONV_LAYERS, D), 0.02),
        'cv_w_pw2': nrm(ks[17], (N_CONV_LAYERS, D, D), D ** -0.5),
        'cv_b_pw2': nrm(ks[18], (N_CONV_LAYERS, D), 0.02),
        'ffn_w_up': nrm(ks[19], (DEPTH, D, F2), D ** -0.5),
        'ffn_w_dw': nrm(ks[20], (DEPTH, FFN_CONV_WIDTH, F2), FFN_CONV_WIDTH ** -0.5),
        'ffn_b_dw': nrm(ks[21], (DEPTH, F2), 0.02),
        'ffn_w_down': nrm(ks[22], (DEPTH, D_FF, D), D_FF ** -0.5),
        'final_norm_g': 1.0 + nrm(ks[23], (D,), 0.1),
    }


def reference(x, c, ctx, c_ctx, w_mod, b_mod, pool_w, pool_scale, na_w_qkv, na_w_o, na_rpb,
              cv_w_pw1, cv_b_pw1, cv_w_dw, cv_b_dw, cv_ln_g, cv_ln_b, cv_w_pw2, cv_b_pw2,
              ffn_w_up, ffn_w_dw, ffn_b_dw, ffn_w_down, final_norm_g):
    for i in range(DEPTH):
        kind, j = i % N_MIXERS, i // N_MIXERS
        update_ctx = i < DEPTH - 1
        need_ctx = update_ctx or kind == 1
        m = _modulation(c, w_mod[i], b_mod[i])
        h = _ada_in(x, m, 0)
        if need_ctx:
            mc = _modulation(c_ctx[None], w_mod[i], b_mod[i])
            hc = _ada_in(ctx, mc, 0)
        if kind == 0:
            y = pool_mixer(h, pool_w[j], pool_scale[j])
            yc = pool_mixer(hc, pool_w[j], pool_scale[j]) if update_ctx else None
        elif kind == 1:
            y, yc = na_mixer(h, hc, na_w_qkv[j], na_w_o[j], na_rpb[j], update_ctx)
        else:
            cv = (cv_w_pw1[j], cv_b_pw1[j], cv_w_dw[j], cv_b_dw[j], cv_ln_g[j], cv_ln_b[j],
                  cv_w_pw2[j], cv_b_pw2[j])
            y = conv_module(h, *cv)
            yc = conv_module(hc, *cv) if update_ctx else None
        x = x + m[:, 2] * y
        ff = (ffn_w_up[i], ffn_w_dw[i], ffn_b_dw[i], ffn_w_down[i])
        x = x + m[:, 5] * conv_ffn(_ada_in(x, m, 3), *ff)
        if update_ctx:
            ctx = ctx + mc[:, 2] * yc
            ctx = ctx + mc[:, 5] * conv_ffn(_ada_in(ctx, mc, 3), *ff)
    return _rms_norm(x) * final_norm_g
```

```python
import functools

import jax
import jax.numpy as jnp
from jax import lax
from jax.experimental import pallas as pl
from jax.experimental.pallas import tpu as pltpu

F32 = jnp.float32
BF16 = jnp.bfloat16

N_MIXERS = 3
N_MOD = 6
POOL_WINDOWS = (2, 4, 8, 16)
GRID_W = 64
NA_HEAD_DIM = 64
WIN_H = 8
WIN_W = 16
NORM_EPS = 1e-6
LN_EPS = 1e-5
NEG_INF = -1e30

LANES = 128
HALO = 16
VMEM_LIMIT = 56 * 1024 * 1024


def _params(*sem):
    return pltpu.CompilerParams(dimension_semantics=sem, vmem_limit_bytes=VMEM_LIMIT)


def _rms(xf):
    return xf * lax.rsqrt(jnp.mean(xf * xf, axis=-1, keepdims=True) + NORM_EPS)


def _ada(xf, shift, scale):
    return _rms(xf) * (1.0 + scale) + shift


def _halo_maps(tm, seq):
    per = tm // HALO
    last = seq // HALO - 1
    prev = lambda b, t, *_: (b, jnp.maximum(t * per - 1, 0), 0)
    nxt = lambda b, t, *_: (b, jnp.minimum((t + 1) * per, last), 0)
    return prev, nxt


def _mod_kernel(cb_ref, w_ref, b_ref, o_ref):
    w = w_ref[...]
    reps = w.shape[1] // LANES
    n_rows = cb_ref.shape[0]
    for r in range(n_rows):
        cb = jax.nn.silu(cb_ref[r])
        o_ref[r:r + 1, :] = jnp.sum(w * pltpu.repeat(cb, reps, axis=1), axis=0, keepdims=True) + b_ref[...]
    o_ref[n_rows:, :] = jnp.zeros((o_ref.shape[0] - n_rows, w.shape[1]), F32)


def _modulation(c_all, w_mod, b_mod, tn=1024):
    depth, d, n = w_mod.shape
    rows = c_all.shape[0]
    cb = jnp.broadcast_to(c_all[:, :, None], (rows, d, LANES))
    return pl.pallas_call(
        _mod_kernel,
        grid=(depth, n // tn),
        in_specs=[
            pl.BlockSpec((rows, d, LANES), lambda l, j: (0, 0, 0)),
            pl.BlockSpec((None, d, tn), lambda l, j: (l, 0, j)),
            pl.BlockSpec((None, 1, tn), lambda l, j: (l, 0, j)),
        ],
        out_specs=pl.BlockSpec((None, 8, tn), lambda l, j: (l, 0, j)),
        out_shape=jax.ShapeDtypeStruct((depth, 8, n), F32),
        compiler_params=_params("parallel", "parallel"),
        name="modulation",
    )(cb, w_mod, b_mod.reshape(depth, 1, n))


def _pool_kernel(x_ref, xp_ref, xn_ref, mod_ref, w_ref, ls_ref, o_ref, hs_ref, *, tm, nt, seq):
    t = pl.program_id(1)
    shift, scale, gate = mod_ref[0:1, :], mod_ref[1:2, :], mod_ref[2:3, :]
    x = x_ref[...]
    hs_ref[HALO:HALO + tm, :] = _ada(x, shift, scale)
    hs_ref[0:HALO, :] = jnp.where(t > 0, _ada(xp_ref[...], shift, scale), 0.0)
    hs_ref[HALO + tm:, :] = jnp.where(t < nt - 1, _ada(xn_ref[...], shift, scale), 0.0)
    grp = x.shape[1] // len(POOL_WINDOWS)
    pos = t * tm + lax.broadcasted_iota(jnp.int32, (tm, grp), 0)
    for g, w in enumerate(POOL_WINDOWS):
        cols = slice(g * grp, (g + 1) * grp)
        start = HALO - w // 2
        s = hs_ref[pl.ds(start, tm), cols]
        for k in range(1, w):
            s = s + hs_ref[pl.ds(start + k, tm), cols]
        cnt = (jnp.clip(pos - w // 2 + w, 0, seq) - jnp.clip(pos - w // 2, 0, seq)).astype(F32)
        d = (s / cnt - hs_ref[HALO:HALO + tm, cols]).astype(BF16)
        y = jnp.dot(d, w_ref[g], preferred_element_type=F32)
        o_ref[:, cols] = x[:, cols] + gate[:, cols] * (y * ls_ref[:, cols])


def _pool_mixer(x, mod, w_grp, ls, tm):
    b, seq, d = x.shape
    nt = seq // tm
    prev, nxt = _halo_maps(tm, seq)
    return pl.pallas_call(
        functools.partial(_pool_kernel, tm=tm, nt=nt, seq=seq),
        grid=(b, nt),
        in_specs=[
            pl.BlockSpec((None, tm, d), lambda i, t: (i, t, 0)),
            pl.BlockSpec((None, HALO, d), prev),
            pl.BlockSpec((None, HALO, d), nxt),
            pl.BlockSpec((None, N_MOD, d), lambda i, t: (i, 0, 0)),
            pl.BlockSpec(w_grp.shape, lambda i, t: (0, 0, 0)),
            pl.BlockSpec((1, d), lambda i, t: (0, 0)),
        ],
        out_specs=pl.BlockSpec((None, tm, d), lambda i, t: (i, t, 0)),
        out_shape=jax.ShapeDtypeStruct(x.shape, F32),
        scratch_shapes=[pltpu.VMEM((tm + 2 * HALO, d), F32)],
        compiler_params=_params("parallel", "parallel"),
        name="pool_mixer",
    )(x, x, x, mod, w_grp, ls)


def _ffn_kernel(x_ref, xp_ref, xn_ref, mod_ref, wg_ref, wv_ref, dwg_ref, dwv_ref, bg_ref, bv_ref,
                wd_ref, fg_ref, o_ref, h_ref, ug_ref, uv_ref, *, tm, nt, final):
    t = pl.program_id(1)
    f = pl.program_id(2)
    shift, scale, gate = mod_ref[3:4, :], mod_ref[4:5, :], mod_ref[5:6, :]

    @pl.when(f == 0)
    def _():
        h_ref[HALO:HALO + tm, :] = _ada(x_ref[...], shift, scale).astype(BF16)
        h_ref[0:HALO, :] = jnp.where(t > 0, _ada(xp_ref[...], shift, scale), 0.0).astype(BF16)
        h_ref[HALO + tm:, :] = jnp.where(t < nt - 1, _ada(xn_ref[...], shift, scale), 0.0).astype(BF16)
        o_ref[...] = jnp.zeros(o_ref.shape, F32)

    h = h_ref[...]
    ug_ref[...] = jnp.dot(h, wg_ref[...], preferred_element_type=F32)
    uv_ref[...] = jnp.dot(h, wv_ref[...], preferred_element_type=F32)

    def conv3(u_ref, dw_ref, b_ref):
        return (dw_ref[0:1, :] * u_ref[pl.ds(HALO - 1, tm), :]
                + dw_ref[1:2, :] * u_ref[pl.ds(HALO, tm), :]
                + dw_ref[2:3, :] * u_ref[pl.ds(HALO + 1, tm), :]
                + b_ref[...])

    act = jax.nn.silu(conv3(ug_ref, dwg_ref, bg_ref)) * conv3(uv_ref, dwv_ref, bv_ref)
    o_ref[...] += jnp.dot(act.astype(BF16), wd_ref[...], preferred_element_type=F32)

    @pl.when(f == pl.num_programs(2) - 1)
    def _():
        xn = x_ref[...] + gate * o_ref[...]
        if final:
            xn = _rms(xn) * fg_ref[...]
        o_ref[...] = xn


def _conv_ffn(x, mod, w_up, w_dw, b_dw, w_down, final_g, tm, fc, final):
    b, seq, d = x.shape
    dff = w_down.shape[0]
    nf = dff // fc
    nt = seq // tm
    prev, nxt = _halo_maps(tm, seq)
    return pl.pallas_call(
        functools.partial(_ffn_kernel, tm=tm, nt=nt, final=final),
        grid=(b, nt, nf),
        in_specs=[
            pl.BlockSpec((None, tm, d), lambda i, t, f: (i, t, 0)),
            pl.BlockSpec((None, HALO, d), prev),
            pl.BlockSpec((None, HALO, d), nxt),
            pl.BlockSpec((None, N_MOD, d), lambda i, t, f: (i, 0, 0)),
            pl.BlockSpec((d, fc), lambda i, t, f: (0, f)),
            pl.BlockSpec((d, fc), lambda i, t, f: (0, nf + f)),
            pl.BlockSpec((w_dw.shape[0], fc), lambda i, t, f: (0, f)),
            pl.BlockSpec((w_dw.shape[0], fc), lambda i, t, f: (0, nf + f)),
            pl.BlockSpec((1, fc), lambda i, t, f: (0, f)),
            pl.BlockSpec((1, fc), lambda i, t, f: (0, nf + f)),
            pl.BlockSpec((fc, d), lambda i, t, f: (f, 0)),
            pl.BlockSpec((1, d), lambda i, t, f: (0, 0)),
        ],
        out_specs=pl.BlockSpec((None, tm, d), lambda i, t, f: (i, t, 0)),
        out_shape=jax.ShapeDtypeStruct(x.shape, F32),
        scratch_shapes=[
            pltpu.VMEM((tm + 2 * HALO, d), BF16),
            pltpu.VMEM((tm + 2 * HALO, fc), F32),
            pltpu.VMEM((tm + 2 * HALO, fc), F32),
        ],
        compiler_params=_params("parallel", "parallel", "arbitrary"),
        name="conv_ffn",
    )(x, x, x, mod, w_up, w_up, w_dw, w_dw, b_dw, b_dw, w_down, final_g)


def _qkv_kernel(x_ref, mod_ref, w_ref, o_ref, h_ref):
    @pl.when(pl.program_id(2) == 0)
    def _():
        h_ref[...] = _ada(x_ref[...], mod_ref[0:1, :], mod_ref[1:2, :]).astype(BF16)

    o_ref[...] = jnp.dot(h_ref[...], w_ref[...], preferred_element_type=F32).astype(o_ref.dtype)


def _qkv_proj(x, mod, w, tm, tn):
    b, seq, d = x.shape
    n = w.shape[1]
    return pl.pallas_call(
        _qkv_kernel,
        grid=(b, seq // tm, n // tn),
        in_specs=[
            pl.BlockSpec((None, tm, d), lambda i, t, j: (i, t, 0)),
            pl.BlockSpec((None, N_MOD, d), lambda i, t, j: (i, 0, 0)),
            pl.BlockSpec((d, tn), lambda i, t, j: (0, j)),
        ],
        out_specs=pl.BlockSpec((None, tm, tn), lambda i, t, j: (i, t, j)),
        out_shape=jax.ShapeDtypeStruct((b, seq, n), BF16),
        scratch_shapes=[pltpu.VMEM((tm, d), BF16)],
        compiler_params=_params("parallel", "parallel", "arbitrary"),
        name="qkv_proj",
    )(x, mod, w)


def _head_pair_queries(q_ref, cols, lo_mask):
    q2 = q_ref[:, cols].astype(F32) * (NA_HEAD_DIM ** -0.5)
    return jnp.concatenate([jnp.where(lo_mask, q2, 0.0), jnp.where(lo_mask, 0.0, q2)], axis=0).astype(BF16)


def _nt_dot(a, b):
    return lax.dot_general(a, b, (((1,), (1,)), ((), ())), preferred_element_type=F32)


def _na_kernel(q_ref, k_ref, v_ref, kc_ref, vc_ref, bias_ref, o_ref):
    nq, d = q_ref.shape
    lo_mask = lax.broadcasted_iota(jnp.int32, (nq, LANES), 1) < NA_HEAD_DIM
    for p in range(d // LANES):
        cols = slice(p * LANES, (p + 1) * LANES)
        qq = _head_pair_queries(q_ref, cols, lo_mask)
        s_loc = _nt_dot(qq, k_ref[0, :, cols]) + bias_ref[p]
        s_ctx = _nt_dot(qq, kc_ref[:, cols])
        m = jnp.maximum(jnp.max(s_loc, axis=-1, keepdims=True), jnp.max(s_ctx, axis=-1, keepdims=True))
        p_loc = jnp.exp(s_loc - m)
        p_ctx = jnp.exp(s_ctx - m)
        denom = jnp.sum(p_loc, axis=-1, keepdims=True) + jnp.sum(p_ctx, axis=-1, keepdims=True)
        r = (jnp.dot(p_loc.astype(BF16), v_ref[0, :, cols], preferred_element_type=F32)
             + jnp.dot(p_ctx.astype(BF16), vc_ref[:, cols], preferred_element_type=F32)) / denom
        o_ref[:, cols] = jnp.where(lo_mask, r[:nq], r[nq:]).astype(o_ref.dtype)


def _bias_tables(rpb):
    heads = rpb.shape[0]
    col = jnp.arange(GRID_W)
    col_start = jnp.clip(col - WIN_W // 2, 0, GRID_W - WIN_W)
    valid = (col[None, :] >= col_start[:, None]) & (col[None, :] < col_start[:, None] + WIN_W)
    col_off = jnp.clip(col[None, :] - col[:, None], -(WIN_W - 1), WIN_W - 1) + (WIN_W - 1)
    onehot = (col_off[None] == jnp.arange(2 * WIN_W - 1)[:, None, None]).astype(F32)
    tb = jnp.einsum('hrc,cqk->hrqk', rpb, onehot, precision=lax.Precision.HIGHEST)
    tb = jnp.where(valid[None, None], tb, NEG_INF)
    variants = []
    for v in range(WIN_H):
        win = tb[:, v:v + WIN_H]
        win = win.transpose(0, 2, 1, 3).reshape(heads // 2, 2 * GRID_W, WIN_H * GRID_W)
        variants.append(win)
    return jnp.stack(variants)


def _na_attention(qkv, qkv_ctx, bias):
    b, seq, d3 = qkv.shape
    d = d3 // 3
    rows = seq // GRID_W
    lc = qkv_ctx.shape[1]
    n_loc = WIN_H * GRID_W

    def r0(r):
        return jnp.clip(r - WIN_H // 2, 0, rows - WIN_H)

    win = (pl.Element(1), pl.Element(n_loc), pl.Element(d))

    return pl.pallas_call(
        _na_kernel,
        grid=(b, rows),
        in_specs=[
            pl.BlockSpec((None, GRID_W, d), lambda i, r: (i, r, 0)),
            pl.BlockSpec(win, lambda i, r: (i, r0(r) * GRID_W, d)),
            pl.BlockSpec(win, lambda i, r: (i, r0(r) * GRID_W, 2 * d)),
            pl.BlockSpec((None, lc, d), lambda i, r: (i, 0, 1)),
            pl.BlockSpec((None, lc, d), lambda i, r: (i, 0, 2)),
            pl.BlockSpec((None,) + bias.shape[1:], lambda i, r: (r0(r) - r + WIN_H - 1, 0, 0, 0)),
        ],
        out_specs=pl.BlockSpec((None, GRID_W, d), lambda i, r: (i, r, 0)),
        out_shape=jax.ShapeDtypeStruct((b, seq, d), BF16),
        compiler_params=_params("parallel", "parallel"),
        name="na_attention",
    )(qkv, qkv, qkv, qkv_ctx, qkv_ctx, bias)


def _ctx_attn_kernel(q_ref, k_ref, v_ref, o_ref):
    nq, d = q_ref.shape
    lo_mask = lax.broadcasted_iota(jnp.int32, (nq, LANES), 1) < NA_HEAD_DIM
    for p in range(d // LANES):
        cols = slice(p * LANES, (p + 1) * LANES)
        qq = _head_pair_queries(q_ref, cols, lo_mask)
        s = _nt_dot(qq, k_ref[:, cols])
        e = jnp.exp(s - jnp.max(s, axis=-1, keepdims=True))
        r = jnp.dot(e.astype(BF16), v_ref[:, cols], preferred_element_type=F32) / jnp.sum(e, axis=-1, keepdims=True)
        o_ref[:, cols] = jnp.where(lo_mask, r[:nq], r[nq:]).astype(o_ref.dtype)


def _ctx_attention(qkv_ctx):
    b, lc, d3 = qkv_ctx.shape
    d = d3 // 3
    return pl.pallas_call(
        _ctx_attn_kernel,
        grid=(b,),
        in_specs=[pl.BlockSpec((None, lc, d), lambda i, j=j: (i, 0, j)) for j in range(3)],
        out_specs=pl.BlockSpec((None, lc, d), lambda i: (i, 0, 0)),
        out_shape=jax.ShapeDtypeStruct((b, lc, d), BF16),
        compiler_params=_params("parallel"),
        name="ctx_attention",
    )(qkv_ctx, qkv_ctx, qkv_ctx)


def _oproj_kernel(a_ref, x_ref, mod_ref, w_ref, o_ref):
    y = jnp.dot(a_ref[...], w_ref[...], preferred_element_type=F32)
    o_ref[...] = x_ref[...] + mod_ref[2:3, :] * y


def _out_proj(a, x, mod, w, tm):
    b, seq, d = x.shape
    return pl.pallas_call(
        _oproj_kernel,
        grid=(b, seq // tm),
        in_specs=[
            pl.BlockSpec((None, tm, d), lambda i, t: (i, t, 0)),
            pl.BlockSpec((None, tm, d), lambda i, t: (i, t, 0)),
            pl.BlockSpec((None, N_MOD, d), lambda i, t: (i, 0, 0)),
            pl.BlockSpec((d, d), lambda i, t: (0, 0)),
        ],
        out_specs=pl.BlockSpec((None, tm, d), lambda i, t: (i, t, 0)),
        out_shape=jax.ShapeDtypeStruct(x.shape, F32),
        compiler_params=_params("parallel", "parallel"),
        name="attn_out_proj",
    )(a, x, mod, w)


def _glu_kernel(x_ref, mod_ref, wa_ref, wg_ref, ba_ref, bg_ref, o_ref, h_ref):
    @pl.when(pl.program_id(2) == 0)
    def _():
        h_ref[...] = _ada(x_ref[...], mod_ref[0:1, :], mod_ref[1:2, :]).astype(BF16)

    h = h_ref[...]
    a = jnp.dot(h, wa_ref[...], preferred_element_type=F32) + ba_ref[...]
    g = jnp.dot(h, wg_ref[...], preferred_element_type=F32) + bg_ref[...]
    o_ref[...] = a * jax.nn.sigmoid(g)


def _glu_proj(x, mod, w, bias, tm, tn):
    b, seq, d = x.shape
    nn = d // tn
    return pl.pallas_call(
        _glu_kernel,
        grid=(b, seq // tm, nn),
        in_specs=[
            pl.BlockSpec((None, tm, d), lambda i, t, j: (i, t, 0)),
            pl.BlockSpec((None, N_MOD, d), lambda i, t, j: (i, 0, 0)),
            pl.BlockSpec((d, tn), lambda i, t, j: (0, j)),
            pl.BlockSpec((d, tn), lambda i, t, j: (0, nn + j)),
            pl.BlockSpec((1, tn), lambda i, t, j: (0, j)),
            pl.BlockSpec((1, tn), lambda i, t, j: (0, nn + j)),
        ],
        out_specs=pl.BlockSpec((None, tm, tn), lambda i, t, j: (i, t, j)),
        out_shape=jax.ShapeDtypeStruct((b, seq, d), F32),
        scratch_shapes=[pltpu.VMEM((tm, d), BF16)],
        compiler_params=_params("parallel", "parallel", "arbitrary"),
        name="conv_glu_proj",
    )(x, mod, w, w, bias, bias)


def _convmod_kernel(u_ref, up_ref, un_ref, x_ref, mod_ref, dw_ref, bdw_ref, lng_ref, lnb_ref,
                    w2_ref, b2_ref, o_ref, us_ref, *, tm, nt):
    t = pl.program_id(1)
    us_ref[HALO:HALO + tm, :] = u_ref[...]
    us_ref[0:HALO, :] = jnp.where(t > 0, up_ref[...], 0.0)
    us_ref[HALO + tm:, :] = jnp.where(t < nt - 1, un_ref[...], 0.0)
    width = dw_ref.shape[0]
    start = HALO - width // 2
    acc = bdw_ref[...] + dw_ref[0:1, :] * us_ref[pl.ds(start, tm), :]
    for k in range(1, width):
        acc = acc + dw_ref[k:k + 1, :] * us_ref[pl.ds(start + k, tm), :]
    mu = jnp.mean(acc, axis=-1, keepdims=True)
    cen = acc - mu
    var = jnp.mean(cen * cen, axis=-1, keepdims=True)
    z = jax.nn.silu(cen * lax.rsqrt(var + LN_EPS) * lng_ref[...] + lnb_ref[...])
    y = jnp.dot(z.astype(BF16), w2_ref[...], preferred_element_type=F32) + b2_ref[...]
    o_ref[...] = x_ref[...] + mod_ref[2:3, :] * y


def _conv_module_tail(u, x, mod, w_dw, b_dw, ln_g, ln_b, w2, b2, tm):
    b, seq, d = x.shape
    nt = seq // tm
    prev, nxt = _halo_maps(tm, seq)
    row = lambda i, t: (0, 0)
    return pl.pallas_call(
        functools.partial(_convmod_kernel, tm=tm, nt=nt),
        grid=(b, nt),
        in_specs=[
            pl.BlockSpec((None, tm, d), lambda i, t: (i, t, 0)),
            pl.BlockSpec((None, HALO, d), prev),
            pl.BlockSpec((None, HALO, d), nxt),
            pl.BlockSpec((None, tm, d), lambda i, t: (i, t, 0)),
            pl.BlockSpec((None, N_MOD, d), lambda i, t: (i, 0, 0)),
            pl.BlockSpec(w_dw.shape, row),
            pl.BlockSpec((1, d), row),
            pl.BlockSpec((1, d), row),
            pl.BlockSpec((1, d), row),
            pl.BlockSpec((d, d), row),
            pl.BlockSpec((1, d), row),
        ],
        out_specs=pl.BlockSpec((None, tm, d), lambda i, t: (i, t, 0)),
        out_shape=jax.ShapeDtypeStruct(x.shape, F32),
        scratch_shapes=[pltpu.VMEM((tm + 2 * HALO, d), F32)],
        compiler_params=_params("parallel", "parallel"),
        name="conv_module_tail",
    )(u, u, u, x, mod, w_dw, b_dw, ln_g, ln_b, w2, b2)


def kernel(x, c, ctx, c_ctx, w_mod, b_mod, pool_w, pool_scale, na_w_qkv, na_w_o, na_rpb, cv_w_pw1, cv_b_pw1, cv_w_dw, cv_b_dw, cv_ln_g, cv_ln_b, cv_w_pw2, cv_b_pw2, ffn_w_up, ffn_w_dw, ffn_b_dw, ffn_w_down, final_norm_g):
    batch, seq, d = x.shape
    depth = w_mod.shape[0]
    lc = ctx.shape[1]
    assert seq % GRID_W == 0 and seq // GRID_W >= WIN_H and d % (2 * LANES) == 0
    tm = min(512, seq)
    tmc = lc
    fc = 512

    c_all = jnp.concatenate([c, c_ctx[None]], axis=0)
    m_all = _modulation(c_all, w_mod, b_mod).reshape(depth, 8, N_MOD, d)
    row = lambda v: v.reshape(1, -1)

    for i in range(depth):
        kind, j = i % N_MIXERS, i // N_MIXERS
        update_ctx = i < depth - 1
        mod = m_all[i, :batch]
        mod_c = jnp.broadcast_to(m_all[i, batch:batch + 1], (batch, N_MOD, d))
        if kind == 0:
            w_grp = pool_w[j].astype(BF16)
            x_mid = _pool_mixer(x, mod, w_grp, row(pool_scale[j]), tm)
            if update_ctx:
                ctx_mid = _pool_mixer(ctx, mod_c, w_grp, row(pool_scale[j]), tmc)
        elif kind == 1:
            w_qkv = na_w_qkv[j].astype(BF16)
            w_o = na_w_o[j].astype(BF16)
            qkv = _qkv_proj(x, mod, w_qkv, tm, 1024)
            qkv_c = _qkv_proj(ctx, mod_c, w_qkv, tmc, 1024)
            attn = _na_attention(qkv, qkv_c, _bias_tables(na_rpb[j]))
            x_mid = _out_proj(attn, x, mod, w_o, tm)
            if update_ctx:
                ctx_mid = _out_proj(_ctx_attention(qkv_c), ctx, mod_c, w_o, tmc)
        else:
            w1 = cv_w_pw1[j].astype(BF16)
            w2 = cv_w_pw2[j].astype(BF16)
            tail = (cv_w_dw[j], row(cv_b_dw[j]), row(cv_ln_g[j]), row(cv_ln_b[j]), w2, row(cv_b_pw2[j]))
            u = _glu_proj(x, mod, w1, row(cv_b_pw1[j]), tm, 512)
            x_mid = _conv_module_tail(u, x, mod, *tail, tm=256)
            if update_ctx:
                u_c = _glu_proj(ctx, mod_c, w1, row(cv_b_pw1[j]), tmc, 512)
                ctx_mid = _conv_module_tail(u_c, ctx, mod_c, *tail, tm=tmc)
        ffn = (ffn_w_up[i].astype(BF16), ffn_w_dw[i], row(ffn_b_dw[i]), ffn_w_down[i].astype(BF16), row(final_norm_g))
        x = _conv_ffn(x_mid, mod, *ffn, tm=tm, fc=fc, final=(i == depth - 1))
        if update_ctx:
            ctx = _conv_ffn(ctx_mid, mod_c, *ffn, tm=tmc, fc=fc, final=False)
    return x
```

```python
import functools

import numpy as np
import jax
import jax.numpy as jnp
from jax import lax
from jax.experimental import pallas as pl
from jax.experimental.pallas import tpu as pltpu

F32 = jnp.float32
BF16 = jnp.bfloat16

N_MIXERS = 3
N_MOD = 6
POOL_WINDOWS = (2, 4, 8, 16)
GRID_W = 64
NA_HEAD_DIM = 64
WIN_H = 8
WIN_W = 16
NORM_EPS = 1e-6
LN_EPS = 1e-5
NEG_INF = -1e30

LANES = 128
HALO = 16
CONV_NQ = 8
NA_ROWS = 2
VMEM_LIMIT = 58 * 1024 * 1024


def _params(*sem):
    return pltpu.CompilerParams(dimension_semantics=sem, vmem_limit_bytes=VMEM_LIMIT)


def _rms(xf):
    return xf * lax.rsqrt(jnp.mean(xf * xf, axis=-1, keepdims=True) + NORM_EPS)


def _ada(xf, shift, scale):
    return _rms(xf) * (1.0 + scale) + shift


def _halo_maps(tm, seq):
    per = tm // HALO
    last = seq // HALO - 1
    prev = lambda b, t, *_: (b, jnp.maximum(t * per - 1, 0), 0)
    nxt = lambda b, t, *_: (b, jnp.minimum((t + 1) * per, last), 0)
    return prev, nxt


def _mod_kernel(cb_ref, w_ref, b_ref, o_ref):
    w = w_ref[...]
    reps = w.shape[1] // LANES
    n_rows = cb_ref.shape[0]
    for r in range(n_rows):
        cb = jax.nn.silu(cb_ref[r])
        o_ref[r:r + 1, :] = jnp.sum(w * pltpu.repeat(cb, reps, axis=1), axis=0, keepdims=True) + b_ref[...]
    o_ref[n_rows:, :] = jnp.zeros((o_ref.shape[0] - n_rows, w.shape[1]), F32)


def _modulation(c_all, w_mod, b_mod, tn=1024):
    depth, d, n = w_mod.shape
    rows = c_all.shape[0]
    cb = jnp.broadcast_to(c_all[:, :, None], (rows, d, LANES))
    return pl.pallas_call(
        _mod_kernel,
        grid=(depth, n // tn),
        in_specs=[
            pl.BlockSpec((rows, d, LANES), lambda l, j: (0, 0, 0)),
            pl.BlockSpec((None, d, tn), lambda l, j: (l, 0, j)),
            pl.BlockSpec((None, 1, tn), lambda l, j: (l, 0, j)),
        ],
        out_specs=pl.BlockSpec((None, 8, tn), lambda l, j: (l, 0, j)),
        out_shape=jax.ShapeDtypeStruct((depth, 8, n), F32),
        compiler_params=_params("parallel", "parallel"),
        name="modulation",
    )(cb, w_mod, b_mod.reshape(depth, 1, n))


def _pool_kernel(x_ref, xp_ref, xn_ref, mod_ref, w_ref, ls_ref, o_ref, hs_ref, *, tm, nt, seq):
    t = pl.program_id(1)
    shift, scale, gate = mod_ref[0:1, :], mod_ref[1:2, :], mod_ref[2:3, :]
    x = x_ref[...]
    hs_ref[HALO:HALO + tm, :] = _ada(x, shift, scale)
    hs_ref[0:HALO, :] = jnp.where(t > 0, _ada(xp_ref[...], shift, scale), 0.0)
    hs_ref[HALO + tm:, :] = jnp.where(t < nt - 1, _ada(xn_ref[...], shift, scale), 0.0)
    grp = x.shape[1] // len(POOL_WINDOWS)
    pos = t * tm + lax.broadcasted_iota(jnp.int32, (tm, grp), 0)
    for g, w in enumerate(POOL_WINDOWS):
        cols = slice(g * grp, (g + 1) * grp)
        start = HALO - w // 2
        s = hs_ref[pl.ds(start, tm), cols]
        for k in range(1, w):
            s = s + hs_ref[pl.ds(start + k, tm), cols]
        cnt = (jnp.clip(pos - w // 2 + w, 0, seq) - jnp.clip(pos - w // 2, 0, seq)).astype(F32)
        d = (s / cnt - hs_ref[HALO:HALO + tm, cols]).astype(BF16)
        y = jnp.dot(d, w_ref[g], preferred_element_type=F32)
        o_ref[:, cols] = x[:, cols] + gate[:, cols] * (y * ls_ref[:, cols])


def _pool_mixer(x, mod, w_grp, ls, tm):
    b, seq, d = x.shape
    nt = seq // tm
    prev, nxt = _halo_maps(tm, seq)
    return pl.pallas_call(
        functools.partial(_pool_kernel, tm=tm, nt=nt, seq=seq),
        grid=(b, nt),
        in_specs=[
            pl.BlockSpec((None, tm, d), lambda i, t: (i, t, 0)),
            pl.BlockSpec((None, HALO, d), prev),
            pl.BlockSpec((None, HALO, d), nxt),
            pl.BlockSpec((None, N_MOD, d), lambda i, t: (i, 0, 0)),
            pl.BlockSpec(w_grp.shape, lambda i, t: (0, 0, 0)),
            pl.BlockSpec((1, d), lambda i, t: (0, 0)),
        ],
        out_specs=pl.BlockSpec((None, tm, d), lambda i, t: (i, t, 0)),
        out_shape=jax.ShapeDtypeStruct(x.shape, F32),
        scratch_shapes=[pltpu.VMEM((tm + 2 * HALO, d), F32)],
        compiler_params=_params("parallel", "parallel"),
        name="pool_mixer",
    )(x, x, x, mod, w_grp, ls)


def _ffn_kernel(x_ref, xp_ref, xn_ref, mod_ref, wg_ref, wv_ref, dwg_ref, dwv_ref, bg_ref, bv_ref,
                wd_ref, fg_ref, o_ref, h_ref, ug_ref, uv_ref, *, tm, nt, final):
    t = pl.program_id(1)
    f = pl.program_id(2)
    shift, scale, gate = mod_ref[3:4, :], mod_ref[4:5, :], mod_ref[5:6, :]

    @pl.when(f == 0)
    def _():
        h_ref[HALO:HALO + tm, :] = _ada(x_ref[...], shift, scale).astype(BF16)
        h_ref[0:HALO, :] = jnp.where(t > 0, _ada(xp_ref[...], shift, scale), 0.0).astype(BF16)
        h_ref[HALO + tm:, :] = jnp.where(t < nt - 1, _ada(xn_ref[...], shift, scale), 0.0).astype(BF16)
        o_ref[...] = jnp.zeros(o_ref.shape, F32)

    h = h_ref[...]
    ug_ref[...] = jnp.dot(h, wg_ref[...], preferred_element_type=F32)
    uv_ref[...] = jnp.dot(h, wv_ref[...], preferred_element_type=F32)

    def conv3(u_ref, dw_ref, b_ref):
        return (dw_ref[0:1, :] * u_ref[pl.ds(HALO - 1, tm), :]
                + dw_ref[1:2, :] * u_ref[pl.ds(HALO, tm), :]
                + dw_ref[2:3, :] * u_ref[pl.ds(HALO + 1, tm), :]
                + b_ref[...])

    act = jax.nn.silu(conv3(ug_ref, dwg_ref, bg_ref)) * conv3(uv_ref, dwv_ref, bv_ref)
    o_ref[...] += jnp.dot(act.astype(BF16), wd_ref[...], preferred_element_type=F32)

    @pl.when(f == pl.num_programs(2) - 1)
    def _():
        xn = x_ref[...] + gate * o_ref[...]
        if final:
            xn = _rms(xn) * fg_ref[...]
        o_ref[...] = xn


def _conv_ffn(x, mod, w_up, w_dw, b_dw, w_down, final_g, tm, fc, final):
    b, seq, d = x.shape
    dff = w_down.shape[0]
    nf = dff // fc
    nt = seq // tm
    prev, nxt = _halo_maps(tm, seq)
    x_mode = dict(pipeline_mode=pl.Buffered(1)) if 4 * tm * d * 4 > VMEM_LIMIT // 2 else {}
    return pl.pallas_call(
        functools.partial(_ffn_kernel, tm=tm, nt=nt, final=final),
        grid=(b, nt, nf),
        in_specs=[
            pl.BlockSpec((None, tm, d), lambda i, t, f: (i, t, 0), **x_mode),
            pl.BlockSpec((None, HALO, d), prev),
            pl.BlockSpec((None, HALO, d), nxt),
            pl.BlockSpec((None, N_MOD, d), lambda i, t, f: (i, 0, 0)),
            pl.BlockSpec((d, fc), lambda i, t, f: (0, f)),
            pl.BlockSpec((d, fc), lambda i, t, f: (0, nf + f)),
            pl.BlockSpec((w_dw.shape[0], fc), lambda i, t, f: (0, f)),
            pl.BlockSpec((w_dw.shape[0], fc), lambda i, t, f: (0, nf + f)),
            pl.BlockSpec((1, fc), lambda i, t, f: (0, f)),
            pl.BlockSpec((1, fc), lambda i, t, f: (0, nf + f)),
            pl.BlockSpec((fc, d), lambda i, t, f: (f, 0)),
            pl.BlockSpec((1, d), lambda i, t, f: (0, 0)),
        ],
        out_specs=pl.BlockSpec((None, tm, d), lambda i, t, f: (i, t, 0)),
        out_shape=jax.ShapeDtypeStruct(x.shape, F32),
        scratch_shapes=[
            pltpu.VMEM((tm + 2 * HALO, d), BF16),
            pltpu.VMEM((tm + 2 * HALO, fc), F32),
            pltpu.VMEM((tm + 2 * HALO, fc), F32),
        ],
        compiler_params=_params("parallel", "parallel", "arbitrary"),
        name="conv_ffn",
    )(x, x, x, mod, w_up, w_up, w_dw, w_dw, b_dw, b_dw, w_down, final_g)


def _qkv_kernel(x_ref, mod_ref, w_ref, o_ref, h_ref):
    @pl.when(pl.program_id(2) == 0)
    def _():
        h_ref[...] = _ada(x_ref[...], mod_ref[0:1, :], mod_ref[1:2, :]).astype(BF16)

    o_ref[...] = jnp.dot(h_ref[...], w_ref[...], preferred_element_type=F32).astype(o_ref.dtype)


def _qkv_proj(x, mod, w, tm, tn):
    b, seq, d = x.shape
    n = w.shape[1]
    return pl.pallas_call(
        _qkv_kernel,
        grid=(b, seq // tm, n // tn),
        in_specs=[
            pl.BlockSpec((None, tm, d), lambda i, t, j: (i, t, 0)),
            pl.BlockSpec((None, N_MOD, d), lambda i, t, j: (i, 0, 0)),
            pl.BlockSpec((d, tn), lambda i, t, j: (0, j)),
        ],
        out_specs=pl.BlockSpec((None, tm, tn), lambda i, t, j: (i, t, j)),
        out_shape=jax.ShapeDtypeStruct((b, seq, n), BF16),
        scratch_shapes=[pltpu.VMEM((tm, d), BF16)],
        compiler_params=_params("parallel", "parallel", "arbitrary"),
        name="qkv_proj",
    )(x, mod, w)


def _head_pair_queries(q_ref, cols, lo_mask):
    q2 = q_ref[:, cols].astype(F32) * (NA_HEAD_DIM ** -0.5)
    return jnp.concatenate([jnp.where(lo_mask, q2, 0.0), jnp.where(lo_mask, 0.0, q2)], axis=0).astype(BF16)


def _nt_dot(a, b):
    return lax.dot_general(a, b, (((1,), (1,)), ((), ())), preferred_element_type=F32)


def _na_kernel(w0_ref, var_ref, q_ref, k_ref, v_ref, kc_ref, vc_ref, bias_ref, o_ref):
    nq, d = q_ref.shape
    lo_mask = lax.broadcasted_iota(jnp.int32, (nq, LANES), 1) < NA_HEAD_DIM
    ones_loc = jnp.ones((k_ref.shape[1], LANES), BF16)
    ones_ctx = jnp.ones((kc_ref.shape[0], LANES), BF16)
    for p in range(d // LANES):
        cols = slice(p * LANES, (p + 1) * LANES)
        qq = _head_pair_queries(q_ref, cols, lo_mask)
        s_loc = _nt_dot(qq, k_ref[0, :, cols]) + bias_ref[p]
        s_ctx = _nt_dot(qq, kc_ref[:, cols])
        m = jnp.maximum(jnp.max(s_loc, axis=-1, keepdims=True), jnp.max(s_ctx, axis=-1, keepdims=True))
        p_loc = jnp.exp(s_loc - m).astype(BF16)
        p_ctx = jnp.exp(s_ctx - m).astype(BF16)
        v2 = jnp.concatenate([v_ref[0, :, cols], ones_loc], axis=1)
        vc2 = jnp.concatenate([vc_ref[:, cols], ones_ctx], axis=1)
        r2 = (jnp.dot(p_loc, v2, preferred_element_type=F32) + jnp.dot(p_ctx, vc2, preferred_element_type=F32))
        r = r2[:, :LANES] / r2[:, LANES:LANES + 1]
        o_ref[:, cols] = jnp.where(lo_mask, r[:nq], r[nq:]).astype(o_ref.dtype)


def _na_plan(rows, rq):
    kr = WIN_H + rq - 1
    keys, w0s, var = [], [], []
    for j in range(rows // rq):
        first = j * rq
        r0 = [min(max(first + ri - WIN_H // 2, 0), rows - WIN_H) for ri in range(rq)]
        w0 = min(r0[0], rows - kr)
        key = (w0 - first,) + tuple(a - w0 for a in r0)
        if key not in keys:
            keys.append(key)
        w0s.append(w0)
        var.append(keys.index(key))
    return keys, np.asarray(w0s, np.int32), np.asarray(var, np.int32)


def _bias_tables(rpb, keys, rq):
    heads = rpb.shape[0]
    kr = WIN_H + rq - 1
    col = jnp.arange(GRID_W)
    col_start = jnp.clip(col - WIN_W // 2, 0, GRID_W - WIN_W)
    valid = (col[None, :] >= col_start[:, None]) & (col[None, :] < col_start[:, None] + WIN_W)
    col_off = jnp.clip(col[None, :] - col[:, None], -(WIN_W - 1), WIN_W - 1) + (WIN_W - 1)
    onehot = (col_off[None] == jnp.arange(2 * WIN_W - 1)[:, None, None]).astype(F32)
    tb = jnp.einsum('hrc,cqk->hrqk', rpb, onehot, precision=lax.Precision.HIGHEST)
    out = []
    for key in keys:
        off, starts = key[0], key[1:]
        per_row = []
        for ri in range(rq):
            krs = np.arange(kr)
            in_win = (krs >= starts[ri]) & (krs < starts[ri] + WIN_H)
            ridx = np.clip(krs + off - ri + WIN_H - 1, 0, 2 * WIN_H - 2)
            t = jnp.where(jnp.asarray(in_win)[None, :, None, None] & valid[None, None], tb[:, ridx], NEG_INF)
            per_row.append(t.transpose(0, 2, 1, 3).reshape(heads, GRID_W, kr * GRID_W))
        t = jnp.stack(per_row, axis=1)
        out.append(t.reshape(heads // 2, 2 * rq * GRID_W, kr * GRID_W))
    return jnp.stack(out)


def _na_attention(qkv, qkv_ctx, rpb, rq):
    b, seq, d3 = qkv.shape
    d = d3 // 3
    rows = seq // GRID_W
    lc = qkv_ctx.shape[1]
    kr = WIN_H + rq - 1
    nq = rq * GRID_W
    keys, w0s, var = _na_plan(rows, rq)
    bias = _bias_tables(rpb, keys, rq)
    win = (pl.Element(1), pl.Element(kr * GRID_W), pl.Element(d))
    grid_spec = pltpu.PrefetchScalarGridSpec(
        num_scalar_prefetch=2,
        grid=(b, rows // rq),
        in_specs=[
            pl.BlockSpec((None, nq, d), lambda i, j, w0, vr: (i, j, 0)),
            pl.BlockSpec(win, lambda i, j, w0, vr: (i, w0[j] * GRID_W, d)),
            pl.BlockSpec(win, lambda i, j, w0, vr: (i, w0[j] * GRID_W, 2 * d)),
            pl.BlockSpec((None, lc, d), lambda i, j, w0, vr: (i, 0, 1)),
            pl.BlockSpec((None, lc, d), lambda i, j, w0, vr: (i, 0, 2)),
            pl.BlockSpec((None,) + bias.shape[1:], lambda i, j, w0, vr: (vr[j], 0, 0, 0)),
        ],
        out_specs=pl.BlockSpec((None, nq, d), lambda i, j, w0, vr: (i, j, 0)),
    )
    return pl.pallas_call(
        _na_kernel,
        grid_spec=grid_spec,
        out_shape=jax.ShapeDtypeStruct((b, seq, d), BF16),
        compiler_params=_params("parallel", "parallel"),
        name="na_attention",
    )(jnp.asarray(w0s), jnp.asarray(var), qkv, qkv, qkv, qkv_ctx, qkv_ctx, bias)


def _ctx_attn_kernel(q_ref, k_ref, v_ref, o_ref):
    nq, d = q_ref.shape
    lo_mask = lax.broadcasted_iota(jnp.int32, (nq, LANES), 1) < NA_HEAD_DIM
    for p in range(d // LANES):
        cols = slice(p * LANES, (p + 1) * LANES)
        qq = _head_pair_queries(q_ref, cols, lo_mask)
        s = _nt_dot(qq, k_ref[:, cols])
        e = jnp.exp(s - jnp.max(s, axis=-1, keepdims=True))
        r = jnp.dot(e.astype(BF16), v_ref[:, cols], preferred_element_type=F32) / jnp.sum(e, axis=-1, keepdims=True)
        o_ref[:, cols] = jnp.where(lo_mask, r[:nq], r[nq:]).astype(o_ref.dtype)


def _ctx_attention(qkv_ctx):
    b, lc, d3 = qkv_ctx.shape
    d = d3 // 3
    return pl.pallas_call(
        _ctx_attn_kernel,
        grid=(b,),
        in_specs=[pl.BlockSpec((None, lc, d), lambda i, j=j: (i, 0, j)) for j in range(3)],
        out_specs=pl.BlockSpec((None, lc, d), lambda i: (i, 0, 0)),
        out_shape=jax.ShapeDtypeStruct((b, lc, d), BF16),
        compiler_params=_params("parallel"),
        name="ctx_attention",
    )(qkv_ctx, qkv_ctx, qkv_ctx)


def _oproj_kernel(a_ref, x_ref, mod_ref, w_ref, o_ref):
    y = jnp.dot(a_ref[...], w_ref[...], preferred_element_type=F32)
    o_ref[...] = x_ref[...] + mod_ref[2:3, :] * y


def _out_proj(a, x, mod, w, tm):
    b, seq, d = x.shape
    return pl.pallas_call(
        _oproj_kernel,
        grid=(b, seq // tm),
        in_specs=[
            pl.BlockSpec((None, tm, d), lambda i, t: (i, t, 0)),
            pl.BlockSpec((None, tm, d), lambda i, t: (i, t, 0)),
            pl.BlockSpec((None, N_MOD, d), lambda i, t: (i, 0, 0)),
            pl.BlockSpec((d, d), lambda i, t: (0, 0)),
        ],
        out_specs=pl.BlockSpec((None, tm, d), lambda i, t: (i, t, 0)),
        out_shape=jax.ShapeDtypeStruct(x.shape, F32),
        compiler_params=_params("parallel", "parallel"),
        name="attn_out_proj",
    )(a, x, mod, w)


def _glu_kernel(x_ref, mod_ref, wa_ref, wg_ref, ba_ref, bg_ref, o_ref, h_ref):
    @pl.when(pl.program_id(2) == 0)
    def _():
        h_ref[...] = _ada(x_ref[...], mod_ref[0:1, :], mod_ref[1:2, :]).astype(BF16)

    h = h_ref[...]
    a = jnp.dot(h, wa_ref[...], preferred_element_type=F32) + ba_ref[...]
    g = jnp.dot(h, wg_ref[...], preferred_element_type=F32) + bg_ref[...]
    o_ref[...] = a * jax.nn.sigmoid(g)


def _glu_proj(x, mod, w, bias, tm, tn):
    b, seq, d = x.shape
    nn = d // tn
    return pl.pallas_call(
        _glu_kernel,
        grid=(b, seq // tm, nn),
        in_specs=[
            pl.BlockSpec((None, tm, d), lambda i, t, j: (i, t, 0)),
            pl.BlockSpec((None, N_MOD, d), lambda i, t, j: (i, 0, 0)),
            pl.BlockSpec((d, tn), lambda i, t, j: (0, j)),
            pl.BlockSpec((d, tn), lambda i, t, j: (0, nn + j)),
            pl.BlockSpec((1, tn), lambda i, t, j: (0, j)),
            pl.BlockSpec((1, tn), lambda i, t, j: (0, nn + j)),
        ],
        out_specs=pl.BlockSpec((None, tm, tn), lambda i, t, j: (i, t, j)),
        out_shape=jax.ShapeDtypeStruct((b, seq, d), F32),
        scratch_shapes=[pltpu.VMEM((tm, d), BF16)],
        compiler_params=_params("parallel", "parallel", "arbitrary"),
        name="conv_glu_proj",
    )(x, mod, w, w, bias, bias)


def _convmod_kernel(u_ref, up_ref, un_ref, x_ref, mod_ref, dw_ref, bdw_ref, lng_ref, lnb_ref,
                    w2_ref, b2_ref, o_ref, us_ref, ys_ref, *, tm, nt):
    t = pl.program_id(1)
    nc = us_ref.shape[0]
    width = dw_ref.shape[1]
    half = width // 2
    sub = 8 * CONV_NQ

    def column_block(c, carry):
        cs = pl.ds(pl.multiple_of(c * LANES, LANES), LANES)
        us_ref[c, HALO:HALO + tm, :] = u_ref[:, cs]
        us_ref[c, 0:HALO, :] = jnp.where(t > 0, up_ref[:, cs], 0.0)
        us_ref[c, HALO + tm:, :] = jnp.where(t < nt - 1, un_ref[:, cs], 0.0)
        for m in range(tm // sub):
            e = jnp.concatenate([us_ref[c, pl.ds(HALO + m * sub - half + j, 8, stride=CONV_NQ), :]
                                 for j in range(CONV_NQ + width - 1)], axis=0)
            acc = bdw_ref[c] + dw_ref[c, 0:1, :] * e[0:sub]
            for k in range(1, width):
                acc = acc + dw_ref[c, k:k + 1, :] * e[8 * k:8 * k + sub]
            for q in range(CONV_NQ):
                ys_ref[c, pl.ds(m * sub + q, 8, stride=CONV_NQ), :] = acc[8 * q:8 * q + 8]
        return carry

    lax.fori_loop(0, nc, column_block, 0)
    acc = jnp.concatenate([ys_ref[c] for c in range(nc)], axis=1)
    mu = jnp.mean(acc, axis=-1, keepdims=True)
    cen = acc - mu
    var = jnp.mean(cen * cen, axis=-1, keepdims=True)
    z = jax.nn.silu(cen * lax.rsqrt(var + LN_EPS) * lng_ref[...] + lnb_ref[...])
    y = jnp.dot(z.astype(BF16), w2_ref[...], preferred_element_type=F32) + b2_ref[...]
    o_ref[...] = x_ref[...] + mod_ref[2:3, :] * y


def _conv_module_tail(u, x, mod, w_dw, b_dw, ln_g, ln_b, w2, b2, tm):
    b, seq, d = x.shape
    nt = seq // tm
    nc = d // LANES
    width = w_dw.shape[0]
    assert width // 2 < HALO and tm % (8 * CONV_NQ) == 0
    prev, nxt = _halo_maps(tm, seq)
    row = lambda i, t: (0, 0)
    blocked = lambda i, t: (0, 0, 0)
    w_dw = w_dw.reshape(width, nc, LANES).transpose(1, 0, 2)
    b_dw = b_dw.reshape(nc, 1, LANES)
    return pl.pallas_call(
        functools.partial(_convmod_kernel, tm=tm, nt=nt),
        grid=(b, nt),
        in_specs=[
            pl.BlockSpec((None, tm, d), lambda i, t: (i, t, 0)),
            pl.BlockSpec((None, HALO, d), prev),
            pl.BlockSpec((None, HALO, d), nxt),
            pl.BlockSpec((None, tm, d), lambda i, t: (i, t, 0)),
            pl.BlockSpec((None, N_MOD, d), lambda i, t: (i, 0, 0)),
            pl.BlockSpec((nc, width, LANES), blocked),
            pl.BlockSpec((nc, 1, LANES), blocked),
            pl.BlockSpec((1, d), row),
            pl.BlockSpec((1, d), row),
            pl.BlockSpec((d, d), row),
            pl.BlockSpec((1, d), row),
        ],
        out_specs=pl.BlockSpec((None, tm, d), lambda i, t: (i, t, 0)),
        out_shape=jax.ShapeDtypeStruct(x.shape, F32),
        scratch_shapes=[pltpu.VMEM((nc, tm + 2 * HALO, LANES), F32), pltpu.VMEM((nc, tm, LANES), F32)],
        compiler_params=_params("parallel", "parallel"),
        name="conv_module_tail",
    )(u, u, u, x, mod, w_dw, b_dw, ln_g, ln_b, w2, b2)


def kernel(x, c, ctx, c_ctx, w_mod, b_mod, pool_w, pool_scale, na_w_qkv, na_w_o, na_rpb, cv_w_pw1, cv_b_pw1, cv_w_dw, cv_b_dw, cv_ln_g, cv_ln_b, cv_w_pw2, cv_b_pw2, ffn_w_up, ffn_w_dw, ffn_b_dw, ffn_w_down, final_norm_g):
    batch, seq, d = x.shape
    depth = w_mod.shape[0]
    lc = ctx.shape[1]
    assert seq % GRID_W == 0 and seq // GRID_W >= WIN_H and d % (2 * LANES) == 0
    tm = min(512, seq)
    tmm = min(1024, seq)
    tmc = lc
    fc = 512

    c_all = jnp.concatenate([c, c_ctx[None]], axis=0)
    m_all = _modulation(c_all, w_mod, b_mod).reshape(depth, 8, N_MOD, d)
    row = lambda v: v.reshape(1, -1)

    for i in range(depth):
        kind, j = i % N_MIXERS, i // N_MIXERS
        update_ctx = i < depth - 1
        mod = m_all[i, :batch]
        mod_c = jnp.broadcast_to(m_all[i, batch:batch + 1], (batch, N_MOD, d))
        if kind == 0:
            w_grp = pool_w[j].astype(BF16)
            x_mid = _pool_mixer(x, mod, w_grp, row(pool_scale[j]), tm)
            if update_ctx:
                ctx_mid = _pool_mixer(ctx, mod_c, w_grp, row(pool_scale[j]), tmc)
        elif kind == 1:
            w_qkv = na_w_qkv[j].astype(BF16)
            w_o = na_w_o[j].astype(BF16)
            qkv = _qkv_proj(x, mod, w_qkv, tmm, 1024)
            qkv_c = _qkv_proj(ctx, mod_c, w_qkv, tmc, 1024)
            attn = _na_attention(qkv, qkv_c, na_rpb[j], NA_ROWS)
            x_mid = _out_proj(attn, x, mod, w_o, tm)
            if update_ctx:
                ctx_mid = _out_proj(_ctx_attention(qkv_c), ctx, mod_c, w_o, tmc)
        else:
            w1 = cv_w_pw1[j].astype(BF16)
            w2 = cv_w_pw2[j].astype(BF16)
            tail = (cv_w_dw[j], row(cv_b_dw[j]), row(cv_ln_g[j]), row(cv_ln_b[j]), w2, row(cv_b_pw2[j]))
            u = _glu_proj(x, mod, w1, row(cv_b_pw1[j]), tmm, 512)
            x_mid = _conv_module_tail(u, x, mod, *tail, tm=256)
            if update_ctx:
                u_c = _glu_proj(ctx, mod_c, w1, row(cv_b_pw1[j]), tmc, 512)
                ctx_mid = _conv_module_tail(u_c, ctx, mod_c, *tail, tm=tmc)
        ffn = (ffn_w_up[i].astype(BF16), ffn_w_dw[i], row(ffn_b_dw[i]), ffn_w_down[i].astype(BF16), row(final_norm_g))
        x = _conv_ffn(x_mid, mod, *ffn, tm=tmm, fc=fc, final=(i == depth - 1))
        if update_ctx:
            ctx = _conv_ffn(ctx_mid, mod_c, *ffn, tm=tmc, fc=fc, final=False)
    return x
```

```python
import functools

import numpy as np
import jax
import jax.numpy as jnp
from jax import lax
from jax.experimental import pallas as pl
from jax.experimental.pallas import tpu as pltpu

F32 = jnp.float32
BF16 = jnp.bfloat16

N_MIXERS = 3
N_MOD = 6
POOL_WINDOWS = (2, 4, 8, 16)
GRID_W = 64
NA_HEAD_DIM = 64
WIN_H = 8
WIN_W = 16
NORM_EPS = 1e-6
LN_EPS = 1e-5
NEG_INF = -1e30

LANES = 128
HALO = 16
CONV_NQ = 8
NA_ROWS = 4
VMEM_LIMIT = 58 * 1024 * 1024


def _params(*sem):
    return pltpu.CompilerParams(dimension_semantics=sem, vmem_limit_bytes=VMEM_LIMIT)


def _rms(xf):
    return xf * lax.rsqrt(jnp.mean(xf * xf, axis=-1, keepdims=True) + NORM_EPS)


def _ada(xf, shift, scale):
    return _rms(xf) * (1.0 + scale) + shift


def _halo_maps(tm, seq):
    per = tm // HALO
    last = seq // HALO - 1
    prev = lambda b, t, *_: (b, jnp.maximum(t * per - 1, 0), 0)
    nxt = lambda b, t, *_: (b, jnp.minimum((t + 1) * per, last), 0)
    return prev, nxt


def _mod_kernel(cb_ref, w_ref, b_ref, o_ref):
    w = w_ref[...]
    reps = w.shape[1] // LANES
    n_rows = cb_ref.shape[0]
    for r in range(n_rows):
        cb = jax.nn.silu(cb_ref[r])
        o_ref[r:r + 1, :] = jnp.sum(w * jnp.tile(cb, (1, reps)), axis=0, keepdims=True) + b_ref[...]
    o_ref[n_rows:, :] = jnp.zeros((o_ref.shape[0] - n_rows, w.shape[1]), F32)


def _modulation(c_all, w_mod, b_mod, tn=2048):
    depth, d, n = w_mod.shape
    rows = c_all.shape[0]
    cb = jnp.broadcast_to(c_all[:, :, None], (rows, d, LANES))
    return pl.pallas_call(
        _mod_kernel,
        grid=(depth, n // tn),
        in_specs=[
            pl.BlockSpec((rows, d, LANES), lambda l, j: (0, 0, 0)),
            pl.BlockSpec((None, d, tn), lambda l, j: (l, 0, j)),
            pl.BlockSpec((None, 1, tn), lambda l, j: (l, 0, j)),
        ],
        out_specs=pl.BlockSpec((None, 8, tn), lambda l, j: (l, 0, j)),
        out_shape=jax.ShapeDtypeStruct((depth, 8, n), F32),
        compiler_params=_params("parallel", "parallel"),
        name="modulation",
    )(cb, w_mod, b_mod.reshape(depth, 1, n))


def _pool_kernel(x_ref, xp_ref, xn_ref, mod_ref, w_ref, ls_ref, o_ref, hs_ref, *, tm, nt, seq):
    t = pl.program_id(1)
    shift, scale, gate = mod_ref[0:1, :], mod_ref[1:2, :], mod_ref[2:3, :]
    x = x_ref[...]
    hs_ref[HALO:HALO + tm, :] = _ada(x, shift, scale)
    hs_ref[0:HALO, :] = jnp.where(t > 0, _ada(xp_ref[...], shift, scale), 0.0)
    hs_ref[HALO + tm:, :] = jnp.where(t < nt - 1, _ada(xn_ref[...], shift, scale), 0.0)
    grp = x.shape[1] // len(POOL_WINDOWS)
    pos = t * tm + lax.broadcasted_iota(jnp.int32, (tm, grp), 0)
    for g, w in enumerate(POOL_WINDOWS):
        cols = slice(g * grp, (g + 1) * grp)
        start = HALO - w // 2
        s = hs_ref[pl.ds(start, tm), cols]
        for k in range(1, w):
            s = s + hs_ref[pl.ds(start + k, tm), cols]
        cnt = (jnp.clip(pos - w // 2 + w, 0, seq) - jnp.clip(pos - w // 2, 0, seq)).astype(F32)
        d = (s / cnt - hs_ref[HALO:HALO + tm, cols]).astype(BF16)
        y = jnp.dot(d, w_ref[g], preferred_element_type=F32)
        o_ref[:, cols] = x[:, cols] + gate[:, cols] * (y * ls_ref[:, cols])


def _pool_mixer(x, mod, w_grp, ls, tm):
    b, seq, d = x.shape
    nt = seq // tm
    prev, nxt = _halo_maps(tm, seq)
    return pl.pallas_call(
        functools.partial(_pool_kernel, tm=tm, nt=nt, seq=seq),
        grid=(b, nt),
        in_specs=[
            pl.BlockSpec((None, tm, d), lambda i, t: (i, t, 0)),
            pl.BlockSpec((None, HALO, d), prev),
            pl.BlockSpec((None, HALO, d), nxt),
            pl.BlockSpec((None, N_MOD, d), lambda i, t: (i, 0, 0)),
            pl.BlockSpec(w_grp.shape, lambda i, t: (0, 0, 0)),
            pl.BlockSpec((1, d), lambda i, t: (0, 0)),
        ],
        out_specs=pl.BlockSpec((None, tm, d), lambda i, t: (i, t, 0)),
        out_shape=jax.ShapeDtypeStruct(x.shape, F32),
        scratch_shapes=[pltpu.VMEM((tm + 2 * HALO, d), F32)],
        compiler_params=_params("parallel", "parallel"),
        name="pool_mixer",
    )(x, x, x, mod, w_grp, ls)


def _ffn_kernel(x_ref, xp_ref, xn_ref, mod_ref, wg_ref, wv_ref, dwg_ref, dwv_ref, bg_ref, bv_ref,
                wd_ref, fg_ref, o_ref, h_ref, ug_ref, uv_ref, *, tm, nt, final):
    t = pl.program_id(1)
    f = pl.program_id(2)
    shift, scale, gate = mod_ref[3:4, :], mod_ref[4:5, :], mod_ref[5:6, :]

    @pl.when(f == 0)
    def _():
        h_ref[0:tm, :] = _ada(x_ref[...], shift, scale).astype(BF16)
        before = jnp.where(t > 0, _ada(xp_ref[...], shift, scale), 0.0)[HALO - 1:HALO]
        after = jnp.where(t < nt - 1, _ada(xn_ref[...], shift, scale), 0.0)[0:1]
        rid = lax.broadcasted_iota(jnp.int32, (HALO, x_ref.shape[1]), 0)
        h_ref[tm:, :] = jnp.where(rid == 0, before, jnp.where(rid == 1, after, 0.0)).astype(BF16)
        o_ref[...] = jnp.zeros(o_ref.shape, F32)

    h = h_ref[...]
    for u_ref, w_ref in ((ug_ref, wg_ref), (uv_ref, wv_ref)):
        u = jnp.dot(h, w_ref[...], preferred_element_type=F32)
        u_ref[HALO:HALO + tm, :] = u[0:tm]
        u_ref[HALO - 1:HALO, :] = u[tm:tm + 1]
        u_ref[HALO + tm:HALO + tm + 1, :] = u[tm + 1:tm + 2]

    def conv3(u_ref, dw_ref, b_ref):
        return (dw_ref[0:1, :] * u_ref[pl.ds(HALO - 1, tm), :]
                + dw_ref[1:2, :] * u_ref[pl.ds(HALO, tm), :]
                + dw_ref[2:3, :] * u_ref[pl.ds(HALO + 1, tm), :]
                + b_ref[...])

    act = jax.nn.silu(conv3(ug_ref, dwg_ref, bg_ref)) * conv3(uv_ref, dwv_ref, bv_ref)
    o_ref[...] += jnp.dot(act.astype(BF16), wd_ref[...], preferred_element_type=F32)

    @pl.when(f == pl.num_programs(2) - 1)
    def _():
        xn = x_ref[...] + gate * o_ref[...]
        if final:
            xn = _rms(xn) * fg_ref[...]
        o_ref[...] = xn


def _conv_ffn(x, mod, w_up, w_dw, b_dw, w_down, final_g, tm, fc, final):
    b, seq, d = x.shape
    dff = w_down.shape[0]
    nf = dff // fc
    nt = seq // tm
    prev, nxt = _halo_maps(tm, seq)
    x_mode = dict(pipeline_mode=pl.Buffered(1)) if 4 * tm * d * 4 > VMEM_LIMIT // 2 else {}
    return pl.pallas_call(
        functools.partial(_ffn_kernel, tm=tm, nt=nt, final=final),
        grid=(b, nt, nf),
        in_specs=[
            pl.BlockSpec((None, tm, d), lambda i, t, f: (i, t, 0), **x_mode),
            pl.BlockSpec((None, HALO, d), prev),
            pl.BlockSpec((None, HALO, d), nxt),
            pl.BlockSpec((None, N_MOD, d), lambda i, t, f: (i, 0, 0)),
            pl.BlockSpec((d, fc), lambda i, t, f: (0, f)),
            pl.BlockSpec((d, fc), lambda i, t, f: (0, nf + f)),
            pl.BlockSpec((w_dw.shape[0], fc), lambda i, t, f: (0, f)),
            pl.BlockSpec((w_dw.shape[0], fc), lambda i, t, f: (0, nf + f)),
            pl.BlockSpec((1, fc), lambda i, t, f: (0, f)),
            pl.BlockSpec((1, fc), lambda i, t, f: (0, nf + f)),
            pl.BlockSpec((fc, d), lambda i, t, f: (f, 0)),
            pl.BlockSpec((1, d), lambda i, t, f: (0, 0)),
        ],
        out_specs=pl.BlockSpec((None, tm, d), lambda i, t, f: (i, t, 0)),
        out_shape=jax.ShapeDtypeStruct(x.shape, F32),
        scratch_shapes=[
            pltpu.VMEM((tm + HALO, d), BF16),
            pltpu.VMEM((tm + 2 * HALO, fc), F32),
            pltpu.VMEM((tm + 2 * HALO, fc), F32),
        ],
        compiler_params=_params("parallel", "parallel", "arbitrary"),
        name="conv_ffn",
    )(x, x, x, mod, w_up, w_up, w_dw, w_dw, b_dw, b_dw, w_down, final_g)


def _qkv_kernel(x_ref, mod_ref, w_ref, o_ref, h_ref):
    @pl.when(pl.program_id(2) == 0)
    def _():
        h_ref[...] = _ada(x_ref[...], mod_ref[0:1, :], mod_ref[1:2, :]).astype(BF16)

    o_ref[...] = jnp.dot(h_ref[...], w_ref[...], preferred_element_type=F32).astype(o_ref.dtype)


def _qkv_proj(x, mod, w, tm, tn):
    b, seq, d = x.shape
    n = w.shape[1]
    return pl.pallas_call(
        _qkv_kernel,
        grid=(b, seq // tm, n // tn),
        in_specs=[
            pl.BlockSpec((None, tm, d), lambda i, t, j: (i, t, 0)),
            pl.BlockSpec((None, N_MOD, d), lambda i, t, j: (i, 0, 0)),
            pl.BlockSpec((d, tn), lambda i, t, j: (0, j)),
        ],
        out_specs=pl.BlockSpec((None, tm, tn), lambda i, t, j: (i, t, j)),
        out_shape=jax.ShapeDtypeStruct((b, seq, n), BF16),
        scratch_shapes=[pltpu.VMEM((tm, d), BF16)],
        compiler_params=_params("parallel", "parallel", "arbitrary"),
        name="qkv_proj",
    )(x, mod, w)


def _head_pair_queries(q_ref, cols, lo_mask):
    q2 = q_ref[:, cols].astype(F32) * (NA_HEAD_DIM ** -0.5)
    return jnp.concatenate([jnp.where(lo_mask, q2, 0.0), jnp.where(lo_mask, 0.0, q2)], axis=0).astype(BF16)


def _nt_dot(a, b):
    return lax.dot_general(a, b, (((1,), (1,)), ((), ())), preferred_element_type=F32)


def _na_kernel(w0_ref, var_ref, q_ref, k_ref, v_ref, kc_ref, vc_ref, bias_ref, o_ref):
    nq, d = q_ref.shape
    lo_mask = lax.broadcasted_iota(jnp.int32, (nq, LANES), 1) < NA_HEAD_DIM
    ones_loc = jnp.ones((k_ref.shape[1], LANES), BF16)
    ones_ctx = jnp.ones((kc_ref.shape[0], LANES), BF16)
    for p in range(d // LANES):
        cols = slice(p * LANES, (p + 1) * LANES)
        qq = _head_pair_queries(q_ref, cols, lo_mask)
        s_loc = _nt_dot(qq, k_ref[0, :, cols]) + bias_ref[p]
        s_ctx = _nt_dot(qq, kc_ref[:, cols])
        m = jnp.maximum(jnp.max(s_loc, axis=-1, keepdims=True), jnp.max(s_ctx, axis=-1, keepdims=True))
        p_loc = jnp.exp(s_loc - m).astype(BF16)
        p_ctx = jnp.exp(s_ctx - m).astype(BF16)
        v2 = jnp.concatenate([v_ref[0, :, cols], ones_loc], axis=1)
        vc2 = jnp.concatenate([vc_ref[:, cols], ones_ctx], axis=1)
        r2 = (jnp.dot(p_loc, v2, preferred_element_type=F32) + jnp.dot(p_ctx, vc2, preferred_element_type=F32))
        r = r2[:, :LANES] / r2[:, LANES:LANES + 1]
        o_ref[:, cols] = jnp.where(lo_mask, r[:nq], r[nq:]).astype(o_ref.dtype)


def _na_plan(rows, rq):
    kr = WIN_H + rq - 1
    keys, w0s, var = [], [], []
    for j in range(rows // rq):
        first = j * rq
        r0 = [min(max(first + ri - WIN_H // 2, 0), rows - WIN_H) for ri in range(rq)]
        w0 = min(r0[0], rows - kr)
        key = (w0 - first,) + tuple(a - w0 for a in r0)
        if key not in keys:
            keys.append(key)
        w0s.append(w0)
        var.append(keys.index(key))
    return keys, np.asarray(w0s, np.int32), np.asarray(var, np.int32)


def _bias_tables(rpb, keys, rq):
    heads = rpb.shape[0]
    kr = WIN_H + rq - 1
    col = jnp.arange(GRID_W)
    col_start = jnp.clip(col - WIN_W // 2, 0, GRID_W - WIN_W)
    valid = (col[None, :] >= col_start[:, None]) & (col[None, :] < col_start[:, None] + WIN_W)
    col_off = jnp.clip(col[None, :] - col[:, None], -(WIN_W - 1), WIN_W - 1) + (WIN_W - 1)
    onehot = (col_off[None] == jnp.arange(2 * WIN_W - 1)[:, None, None]).astype(F32)
    tb = jnp.einsum('hrc,cqk->hrqk', rpb, onehot, precision=lax.Precision.HIGHEST)
    out = []
    for key in keys:
        off, starts = key[0], key[1:]
        per_row = []
        for ri in range(rq):
            krs = np.arange(kr)
            in_win = (krs >= starts[ri]) & (krs < starts[ri] + WIN_H)
            ridx = np.clip(krs + off - ri + WIN_H - 1, 0, 2 * WIN_H - 2)
            t = jnp.where(jnp.asarray(in_win)[None, :, None, None] & valid[None, None], tb[:, ridx], NEG_INF)
            per_row.append(t.transpose(0, 2, 1, 3).reshape(heads, GRID_W, kr * GRID_W))
        t = jnp.stack(per_row, axis=1)
        out.append(t.reshape(heads // 2, 2 * rq * GRID_W, kr * GRID_W))
    return jnp.stack(out)


def _na_attention(qkv, qkv_ctx, rpb, rq):
    b, seq, d3 = qkv.shape
    d = d3 // 3
    rows = seq // GRID_W
    lc = qkv_ctx.shape[1]
    kr = WIN_H + rq - 1
    nq = rq * GRID_W
    keys, w0s, var = _na_plan(rows, rq)
    bias = _bias_tables(rpb, keys, rq)
    win = (pl.Element(1), pl.Element(kr * GRID_W), pl.Element(d))
    grid_spec = pltpu.PrefetchScalarGridSpec(
        num_scalar_prefetch=2,
        grid=(b, rows // rq),
        in_specs=[
            pl.BlockSpec((None, nq, d), lambda i, j, w0, vr: (i, j, 0)),
            pl.BlockSpec(win, lambda i, j, w0, vr: (i, w0[j] * GRID_W, d)),
            pl.BlockSpec(win, lambda i, j, w0, vr: (i, w0[j] * GRID_W, 2 * d)),
            pl.BlockSpec((None, lc, d), lambda i, j, w0, vr: (i, 0, 1)),
            pl.BlockSpec((None, lc, d), lambda i, j, w0, vr: (i, 0, 2)),
            pl.BlockSpec((None,) + bias.shape[1:], lambda i, j, w0, vr: (vr[j], 0, 0, 0),
                         pipeline_mode=pl.Buffered(1)),
        ],
        out_specs=pl.BlockSpec((None, nq, d), lambda i, j, w0, vr: (i, j, 0)),
    )
    return pl.pallas_call(
        _na_kernel,
        grid_spec=grid_spec,
        out_shape=jax.ShapeDtypeStruct((b, seq, d), BF16),
        compiler_params=_params("parallel", "parallel"),
        name="na_attention",
    )(jnp.asarray(w0s), jnp.asarray(var), qkv, qkv, qkv, qkv_ctx, qkv_ctx, bias)


def _ctx_attn_kernel(q_ref, k_ref, v_ref, o_ref):
    nq, d = q_ref.shape
    lo_mask = lax.broadcasted_iota(jnp.int32, (nq, LANES), 1) < NA_HEAD_DIM
    for p in range(d // LANES):
        cols = slice(p * LANES, (p + 1) * LANES)
        qq = _head_pair_queries(q_ref, cols, lo_mask)
        s = _nt_dot(qq, k_ref[:, cols])
        e = jnp.exp(s - jnp.max(s, axis=-1, keepdims=True))
        r = jnp.dot(e.astype(BF16), v_ref[:, cols], preferred_element_type=F32) / jnp.sum(e, axis=-1, keepdims=True)
        o_ref[:, cols] = jnp.where(lo_mask, r[:nq], r[nq:]).astype(o_ref.dtype)


def _ctx_attention(qkv_ctx):
    b, lc, d3 = qkv_ctx.shape
    d = d3 // 3
    return pl.pallas_call(
        _ctx_attn_kernel,
        grid=(b,),
        in_specs=[pl.BlockSpec((None, lc, d), lambda i, j=j: (i, 0, j)) for j in range(3)],
        out_specs=pl.BlockSpec((None, lc, d), lambda i: (i, 0, 0)),
        out_shape=jax.ShapeDtypeStruct((b, lc, d), BF16),
        compiler_params=_params("parallel"),
        name="ctx_attention",
    )(qkv_ctx, qkv_ctx, qkv_ctx)


def _oproj_kernel(a_ref, x_ref, mod_ref, w_ref, o_ref):
    y = jnp.dot(a_ref[...], w_ref[...], preferred_element_type=F32)
    o_ref[...] = x_ref[...] + mod_ref[2:3, :] * y


def _out_proj(a, x, mod, w, tm):
    b, seq, d = x.shape
    return pl.pallas_call(
        _oproj_kernel,
        grid=(b, seq // tm),
        in_specs=[
            pl.BlockSpec((None, tm, d), lambda i, t: (i, t, 0)),
            pl.BlockSpec((None, tm, d), lambda i, t: (i, t, 0)),
            pl.BlockSpec((None, N_MOD, d), lambda i, t: (i, 0, 0)),
            pl.BlockSpec((d, d), lambda i, t: (0, 0)),
        ],
        out_specs=pl.BlockSpec((None, tm, d), lambda i, t: (i, t, 0)),
        out_shape=jax.ShapeDtypeStruct(x.shape, F32),
        compiler_params=_params("parallel", "parallel"),
        name="attn_out_proj",
    )(a, x, mod, w)


def _glu_kernel(x_ref, mod_ref, wa_ref, wg_ref, ba_ref, bg_ref, o_ref, h_ref):
    @pl.when(pl.program_id(2) == 0)
    def _():
        h_ref[...] = _ada(x_ref[...], mod_ref[0:1, :], mod_ref[1:2, :]).astype(BF16)

    h = h_ref[...]
    a = jnp.dot(h, wa_ref[...], preferred_element_type=F32) + ba_ref[...]
    g = jnp.dot(h, wg_ref[...], preferred_element_type=F32) + bg_ref[...]
    o_ref[...] = a * jax.nn.sigmoid(g)


def _glu_proj(x, mod, w, bias, tm, tn):
    b, seq, d = x.shape
    nn = d // tn
    return pl.pallas_call(
        _glu_kernel,
        grid=(b, seq // tm, nn),
        in_specs=[
            pl.BlockSpec((None, tm, d), lambda i, t, j: (i, t, 0)),
            pl.BlockSpec((None, N_MOD, d), lambda i, t, j: (i, 0, 0)),
            pl.BlockSpec((d, tn), lambda i, t, j: (0, j)),
            pl.BlockSpec((d, tn), lambda i, t, j: (0, nn + j)),
            pl.BlockSpec((1, tn), lambda i, t, j: (0, j)),
            pl.BlockSpec((1, tn), lambda i, t, j: (0, nn + j)),
        ],
        out_specs=pl.BlockSpec((None, tm, tn), lambda i, t, j: (i, t, j)),
        out_shape=jax.ShapeDtypeStruct((b, seq, d), F32),
        scratch_shapes=[pltpu.VMEM((tm, d), BF16)],
        compiler_params=_params("parallel", "parallel", "arbitrary"),
        name="conv_glu_proj",
    )(x, mod, w, w, bias, bias)


def _convmod_kernel(u_ref, up_ref, un_ref, x_ref, mod_ref, dw_ref, bdw_ref, lng_ref, lnb_ref,
                    w2_ref, b2_ref, o_ref, us_ref, ys_ref, *, tm, nt):
    t = pl.program_id(1)
    nc = us_ref.shape[0]
    width = dw_ref.shape[1]
    half = width // 2
    sub = 8 * CONV_NQ

    def column_block(c, carry):
        cs = pl.ds(pl.multiple_of(c * LANES, LANES), LANES)
        us_ref[c, HALO:HALO + tm, :] = u_ref[:, cs]
        us_ref[c, 0:HALO, :] = jnp.where(t > 0, up_ref[:, cs], 0.0)
        us_ref[c, HALO + tm:, :] = jnp.where(t < nt - 1, un_ref[:, cs], 0.0)
        for m in range(tm // sub):
            e = jnp.concatenate([us_ref[c, pl.ds(HALO + m * sub - half + j, 8, stride=CONV_NQ), :]
                                 for j in range(CONV_NQ + width - 1)], axis=0)
            acc = bdw_ref[c] + dw_ref[c, 0:1, :] * e[0:sub]
            for k in range(1, width):
                acc = acc + dw_ref[c, k:k + 1, :] * e[8 * k:8 * k + sub]
            for q in range(CONV_NQ):
                ys_ref[c, pl.ds(m * sub + q, 8, stride=CONV_NQ), :] = acc[8 * q:8 * q + 8]
        return carry

    lax.fori_loop(0, nc, column_block, 0)
    acc = jnp.concatenate([ys_ref[c] for c in range(nc)], axis=1)
    mu = jnp.mean(acc, axis=-1, keepdims=True)
    cen = acc - mu
    var = jnp.mean(cen * cen, axis=-1, keepdims=True)
    z = jax.nn.silu(cen * lax.rsqrt(var + LN_EPS) * lng_ref[...] + lnb_ref[...])
    y = jnp.dot(z.astype(BF16), w2_ref[...], preferred_element_type=F32) + b2_ref[...]
    o_ref[...] = x_ref[...] + mod_ref[2:3, :] * y


def _conv_module_tail(u, x, mod, w_dw, b_dw, ln_g, ln_b, w2, b2, tm):
    b, seq, d = x.shape
    nt = seq // tm
    nc = d // LANES
    width = w_dw.shape[0]
    assert width // 2 < HALO and tm % (8 * CONV_NQ) == 0
    prev, nxt = _halo_maps(tm, seq)
    row = lambda i, t: (0, 0)
    blocked = lambda i, t: (0, 0, 0)
    w_dw = w_dw.reshape(width, nc, LANES).transpose(1, 0, 2)
    b_dw = b_dw.reshape(nc, 1, LANES)
    return pl.pallas_call(
        functools.partial(_convmod_kernel, tm=tm, nt=nt),
        grid=(b, nt),
        in_specs=[
            pl.BlockSpec((None, tm, d), lambda i, t: (i, t, 0)),
            pl.BlockSpec((None, HALO, d), prev),
            pl.BlockSpec((None, HALO, d), nxt),
            pl.BlockSpec((None, tm, d), lambda i, t: (i, t, 0)),
            pl.BlockSpec((None, N_MOD, d), lambda i, t: (i, 0, 0)),
            pl.BlockSpec((nc, width, LANES), blocked),
            pl.BlockSpec((nc, 1, LANES), blocked),
            pl.BlockSpec((1, d), row),
            pl.BlockSpec((1, d), row),
            pl.BlockSpec((d, d), row),
            pl.BlockSpec((1, d), row),
        ],
        out_specs=pl.BlockSpec((None, tm, d), lambda i, t: (i, t, 0)),
        out_shape=jax.ShapeDtypeStruct(x.shape, F32),
        scratch_shapes=[pltpu.VMEM((nc, tm + 2 * HALO, LANES), F32), pltpu.VMEM((nc, tm, LANES), F32)],
        compiler_params=_params("parallel", "parallel"),
        name="conv_module_tail",
    )(u, u, u, x, mod, w_dw, b_dw, ln_g, ln_b, w2, b2)


def kernel(x, c, ctx, c_ctx, w_mod, b_mod, pool_w, pool_scale, na_w_qkv, na_w_o, na_rpb, cv_w_pw1, cv_b_pw1, cv_w_dw, cv_b_dw, cv_ln_g, cv_ln_b, cv_w_pw2, cv_b_pw2, ffn_w_up, ffn_w_dw, ffn_b_dw, ffn_w_down, final_norm_g):
    batch, seq, d = x.shape
    depth = w_mod.shape[0]
    lc = ctx.shape[1]
    assert seq % GRID_W == 0 and seq // GRID_W >= WIN_H and d % (2 * LANES) == 0
    tm = min(512, seq)
    tmm = min(1024, seq)
    tmc = lc
    fc = 512

    c_all = jnp.concatenate([c, c_ctx[None]], axis=0)
    m_all = _modulation(c_all, w_mod, b_mod).reshape(depth, 8, N_MOD, d)
    row = lambda v: v.reshape(1, -1)

    for i in range(depth):
        kind, j = i % N_MIXERS, i // N_MIXERS
        update_ctx = i < depth - 1
        mod = m_all[i, :batch]
        mod_c = jnp.broadcast_to(m_all[i, batch:batch + 1], (batch, N_MOD, d))
        if kind == 0:
            w_grp = pool_w[j].astype(BF16)
            x_mid = _pool_mixer(x, mod, w_grp, row(pool_scale[j]), tm)
            if update_ctx:
                ctx_mid = _pool_mixer(ctx, mod_c, w_grp, row(pool_scale[j]), tmc)
        elif kind == 1:
            w_qkv = na_w_qkv[j].astype(BF16)
            w_o = na_w_o[j].astype(BF16)
            qkv = _qkv_proj(x, mod, w_qkv, tmm, 1024)
            qkv_c = _qkv_proj(ctx, mod_c, w_qkv, tmc, 1024)
            attn = _na_attention(qkv, qkv_c, na_rpb[j], NA_ROWS)
            x_mid = _out_proj(attn, x, mod, w_o, tm)
            if update_ctx:
                ctx_mid = _out_proj(_ctx_attention(qkv_c), ctx, mod_c, w_o, tmc)
        else:
            w1 = cv_w_pw1[j].astype(BF16)
            w2 = cv_w_pw2[j].astype(BF16)
            tail = (cv_w_dw[j], row(cv_b_dw[j]), row(cv_ln_g[j]), row(cv_ln_b[j]), w2, row(cv_b_pw2[j]))
            u = _glu_proj(x, mod, w1, row(cv_b_pw1[j]), tmm, 512)
            x_mid = _conv_module_tail(u, x, mod, *tail, tm=256)
            if update_ctx:
                u_c = _glu_proj(ctx, mod_c, w1, row(cv_b_pw1[j]), tmc, 512)
                ctx_mid = _conv_module_tail(u_c, ctx, mod_c, *tail, tm=tmc)
        ffn = (ffn_w_up[i].astype(BF16), ffn_w_dw[i], row(ffn_b_dw[i]), ffn_w_down[i].astype(BF16), row(final_norm_g))
        x = _conv_ffn(x_mid, mod, *ffn, tm=tm, fc=fc, final=(i == depth - 1))
        if update_ctx:
            ctx = _conv_ffn(ctx_mid, mod_c, *ffn, tm=tmc, fc=fc, final=False)
    return x
```

```python
import functools

import numpy as np
import jax
import jax.numpy as jnp
from jax import lax
from jax.experimental import pallas as pl
from jax.experimental.pallas import tpu as pltpu

F32 = jnp.float32
BF16 = jnp.bfloat16

N_MIXERS = 3
N_MOD = 6
POOL_WINDOWS = (2, 4, 8, 16)
GRID_W = 64
NA_HEAD_DIM = 64
WIN_H = 8
WIN_W = 16
NORM_EPS = 1e-6
LN_EPS = 1e-5
NEG_INF = -1e30

LANES = 128
HALO = 16
CONV_NQ = 8
NA_ROWS = 4
VMEM_LIMIT = 58 * 1024 * 1024


def _params(*sem):
    return pltpu.CompilerParams(dimension_semantics=sem, vmem_limit_bytes=VMEM_LIMIT)


def _rms(xf):
    return xf * lax.rsqrt(jnp.mean(xf * xf, axis=-1, keepdims=True) + NORM_EPS)


def _ada(xf, shift, scale):
    return _rms(xf) * (1.0 + scale) + shift


def _halo_maps(tm, seq):
    per = tm // HALO
    last = seq // HALO - 1
    prev = lambda b, t, *_: (b, jnp.maximum(t * per - 1, 0), 0)
    nxt = lambda b, t, *_: (b, jnp.minimum((t + 1) * per, last), 0)
    return prev, nxt


def _mod_kernel(cb_ref, w_ref, b_ref, o_ref):
    w = w_ref[...]
    reps = w.shape[1] // LANES
    n_rows = cb_ref.shape[0]
    for r in range(n_rows):
        cb = jax.nn.silu(cb_ref[r])
        o_ref[r:r + 1, :] = jnp.sum(w * jnp.tile(cb, (1, reps)), axis=0, keepdims=True) + b_ref[...]
    o_ref[n_rows:, :] = jnp.zeros((o_ref.shape[0] - n_rows, w.shape[1]), F32)


def _modulation(c_all, w_mod, b_mod, tn=2048):
    depth, d, n = w_mod.shape
    rows = c_all.shape[0]
    cb = jnp.broadcast_to(c_all[:, :, None], (rows, d, LANES))
    return pl.pallas_call(
        _mod_kernel,
        grid=(depth, n // tn),
        in_specs=[
            pl.BlockSpec((rows, d, LANES), lambda l, j: (0, 0, 0)),
            pl.BlockSpec((None, d, tn), lambda l, j: (l, 0, j)),
            pl.BlockSpec((None, 1, tn), lambda l, j: (l, 0, j)),
        ],
        out_specs=pl.BlockSpec((None, 8, tn), lambda l, j: (l, 0, j)),
        out_shape=jax.ShapeDtypeStruct((depth, 8, n), F32),
        compiler_params=_params("parallel", "parallel"),
        name="modulation",
    )(cb, w_mod, b_mod.reshape(depth, 1, n))


def _pool_kernel(x_ref, xp_ref, xn_ref, mod_ref, w_ref, ls_ref, o_ref, hs_ref, *, tm, nt, seq):
    t = pl.program_id(1)
    shift, scale, gate = mod_ref[0:1, :], mod_ref[1:2, :], mod_ref[2:3, :]
    x = x_ref[...]
    hs_ref[HALO:HALO + tm, :] = _ada(x, shift, scale)
    hs_ref[0:HALO, :] = jnp.where(t > 0, _ada(xp_ref[...], shift, scale), 0.0)
    hs_ref[HALO + tm:, :] = jnp.where(t < nt - 1, _ada(xn_ref[...], shift, scale), 0.0)
    grp = x.shape[1] // len(POOL_WINDOWS)
    pos = t * tm + lax.broadcasted_iota(jnp.int32, (tm, grp), 0)
    for g, w in enumerate(POOL_WINDOWS):
        cols = slice(g * grp, (g + 1) * grp)
        start = HALO - w // 2
        s = hs_ref[pl.ds(start, tm), cols]
        for k in range(1, w):
            s = s + hs_ref[pl.ds(start + k, tm), cols]
        cnt = (jnp.clip(pos - w // 2 + w, 0, seq) - jnp.clip(pos - w // 2, 0, seq)).astype(F32)
        d = (s / cnt - hs_ref[HALO:HALO + tm, cols]).astype(BF16)
        y = jnp.dot(d, w_ref[g], preferred_element_type=F32)
        o_ref[:, cols] = x[:, cols] + gate[:, cols] * (y * ls_ref[:, cols])


def _pool_mixer(x, mod, w_grp, ls, tm):
    b, seq, d = x.shape
    nt = seq // tm
    prev, nxt = _halo_maps(tm, seq)
    return pl.pallas_call(
        functools.partial(_pool_kernel, tm=tm, nt=nt, seq=seq),
        grid=(b, nt),
        in_specs=[
            pl.BlockSpec((None, tm, d), lambda i, t: (i, t, 0)),
            pl.BlockSpec((None, HALO, d), prev),
            pl.BlockSpec((None, HALO, d), nxt),
            pl.BlockSpec((None, N_MOD, d), lambda i, t: (i, 0, 0)),
            pl.BlockSpec(w_grp.shape, lambda i, t: (0, 0, 0)),
            pl.BlockSpec((1, d), lambda i, t: (0, 0)),
        ],
        out_specs=pl.BlockSpec((None, tm, d), lambda i, t: (i, t, 0)),
        out_shape=jax.ShapeDtypeStruct(x.shape, F32),
        scratch_shapes=[pltpu.VMEM((tm + 2 * HALO, d), F32)],
        compiler_params=_params("parallel", "parallel"),
        name="pool_mixer",
    )(x, x, x, mod, w_grp, ls)


def _ffn_kernel(x_ref, xp_ref, xn_ref, mod_ref, wg_ref, wv_ref, dwg_ref, dwv_ref, bg_ref, bv_ref,
                wd_ref, fg_ref, o_ref, h_ref, ug_ref, uv_ref, *, tm, nt, final):
    t = pl.program_id(1)
    f = pl.program_id(2)
    shift, scale, gate = mod_ref[3:4, :], mod_ref[4:5, :], mod_ref[5:6, :]

    @pl.when(f == 0)
    def _():
        h_ref[HALO:HALO + tm, :] = _ada(x_ref[...], shift, scale).astype(BF16)
        h_ref[0:HALO, :] = jnp.where(t > 0, _ada(xp_ref[...], shift, scale), 0.0).astype(BF16)
        h_ref[HALO + tm:, :] = jnp.where(t < nt - 1, _ada(xn_ref[...], shift, scale), 0.0).astype(BF16)
        o_ref[...] = jnp.zeros(o_ref.shape, F32)

    h = h_ref[...]
    ug_ref[...] = jnp.dot(h, wg_ref[...], preferred_element_type=F32)
    uv_ref[...] = jnp.dot(h, wv_ref[...], preferred_element_type=F32)

    def conv3(u_ref, dw_ref, b_ref):
        return (dw_ref[0:1, :] * u_ref[pl.ds(HALO - 1, tm), :]
                + dw_ref[1:2, :] * u_ref[pl.ds(HALO, tm), :]
                + dw_ref[2:3, :] * u_ref[pl.ds(HALO + 1, tm), :]
                + b_ref[...])

    act = jax.nn.silu(conv3(ug_ref, dwg_ref, bg_ref)) * conv3(uv_ref, dwv_ref, bv_ref)
    o_ref[...] += jnp.dot(act.astype(BF16), wd_ref[...], preferred_element_type=F32)

    @pl.when(f == pl.num_programs(2) - 1)
    def _():
        xn = x_ref[...] + gate * o_ref[...]
        if final:
            xn = _rms(xn) * fg_ref[...]
        o_ref[...] = xn


def _conv_ffn(x, mod, w_up, w_dw, b_dw, w_down, final_g, tm, fc, final):
    b, seq, d = x.shape
    dff = w_down.shape[0]
    nf = dff // fc
    nt = seq // tm
    prev, nxt = _halo_maps(tm, seq)
    x_mode = dict(pipeline_mode=pl.Buffered(1)) if 4 * tm * d * 4 > VMEM_LIMIT // 2 else {}
    return pl.pallas_call(
        functools.partial(_ffn_kernel, tm=tm, nt=nt, final=final),
        grid=(b, nt, nf),
        in_specs=[
            pl.BlockSpec((None, tm, d), lambda i, t, f: (i, t, 0), **x_mode),
            pl.BlockSpec((None, HALO, d), prev),
            pl.BlockSpec((None, HALO, d), nxt),
            pl.BlockSpec((None, N_MOD, d), lambda i, t, f: (i, 0, 0)),
            pl.BlockSpec((d, fc), lambda i, t, f: (0, f)),
            pl.BlockSpec((d, fc), lambda i, t, f: (0, nf + f)),
            pl.BlockSpec((w_dw.shape[0], fc), lambda i, t, f: (0, f)),
            pl.BlockSpec((w_dw.shape[0], fc), lambda i, t, f: (0, nf + f)),
            pl.BlockSpec((1, fc), lambda i, t, f: (0, f)),
            pl.BlockSpec((1, fc), lambda i, t, f: (0, nf + f)),
            pl.BlockSpec((fc, d), lambda i, t, f: (f, 0)),
            pl.BlockSpec((1, d), lambda i, t, f: (0, 0)),
        ],
        out_specs=pl.BlockSpec((None, tm, d), lambda i, t, f: (i, t, 0)),
        out_shape=jax.ShapeDtypeStruct(x.shape, F32),
        scratch_shapes=[
            pltpu.VMEM((tm + 2 * HALO, d), BF16),
            pltpu.VMEM((tm + 2 * HALO, fc), F32),
            pltpu.VMEM((tm + 2 * HALO, fc), F32),
        ],
        compiler_params=_params("parallel", "parallel", "arbitrary"),
        name="conv_ffn",
    )(x, x, x, mod, w_up, w_up, w_dw, w_dw, b_dw, b_dw, w_down, final_g)


def _qkv_kernel(x_ref, mod_ref, w_ref, o_ref, h_ref):
    @pl.when(pl.program_id(2) == 0)
    def _():
        h_ref[...] = _ada(x_ref[...], mod_ref[0:1, :], mod_ref[1:2, :]).astype(BF16)

    o_ref[...] = jnp.dot(h_ref[...], w_ref[...], preferred_element_type=F32).astype(o_ref.dtype)


def _qkv_proj(x, mod, w, tm, tn):
    b, seq, d = x.shape
    n = w.shape[1]
    return pl.pallas_call(
        _qkv_kernel,
        grid=(b, seq // tm, n // tn),
        in_specs=[
            pl.BlockSpec((None, tm, d), lambda i, t, j: (i, t, 0)),
            pl.BlockSpec((None, N_MOD, d), lambda i, t, j: (i, 0, 0)),
            pl.BlockSpec((d, tn), lambda i, t, j: (0, j)),
        ],
        out_specs=pl.BlockSpec((None, tm, tn), lambda i, t, j: (i, t, j)),
        out_shape=jax.ShapeDtypeStruct((b, seq, n), BF16),
        scratch_shapes=[pltpu.VMEM((tm, d), BF16)],
        compiler_params=_params("parallel", "parallel", "arbitrary"),
        name="qkv_proj",
    )(x, mod, w)


def _head_pair_queries(q_ref, cols, lo_mask):
    q2 = q_ref[:, cols].astype(F32) * (NA_HEAD_DIM ** -0.5)
    return jnp.concatenate([jnp.where(lo_mask, q2, 0.0), jnp.where(lo_mask, 0.0, q2)], axis=0).astype(BF16)


def _nt_dot(a, b):
    return lax.dot_general(a, b, (((1,), (1,)), ((), ())), preferred_element_type=F32)


def _na_kernel(w0_ref, var_ref, q_ref, k_ref, v_ref, kc_ref, vc_ref, bias_ref, o_ref):
    nq, d = q_ref.shape
    lo_mask = lax.broadcasted_iota(jnp.int32, (nq, LANES), 1) < NA_HEAD_DIM
    ones_loc = jnp.ones((k_ref.shape[1], LANES), BF16)
    ones_ctx = jnp.ones((kc_ref.shape[0], LANES), BF16)
    for p in range(d // LANES):
        cols = slice(p * LANES, (p + 1) * LANES)
        qq = _head_pair_queries(q_ref, cols, lo_mask)
        s_loc = _nt_dot(qq, k_ref[0, :, cols]) + bias_ref[p]
        s_ctx = _nt_dot(qq, kc_ref[:, cols])
        m = jnp.maximum(jnp.max(s_loc, axis=-1, keepdims=True), jnp.max(s_ctx, axis=-1, keepdims=True))
        p_loc = jnp.exp(s_loc - m).astype(BF16)
        p_ctx = jnp.exp(s_ctx - m).astype(BF16)
        v2 = jnp.concatenate([v_ref[0, :, cols], ones_loc], axis=1)
        vc2 = jnp.concatenate([vc_ref[:, cols], ones_ctx], axis=1)
        r2 = (jnp.dot(p_loc, v2, preferred_element_type=F32) + jnp.dot(p_ctx, vc2, preferred_element_type=F32))
        r = r2[:, :LANES] / r2[:, LANES:LANES + 1]
        o_ref[:, cols] = jnp.where(lo_mask, r[:nq], r[nq:]).astype(o_ref.dtype)


def _na_plan(rows, rq):
    kr = WIN_H + rq - 1
    keys, w0s, var = [], [], []
    for j in range(rows // rq):
        first = j * rq
        r0 = [min(max(first + ri - WIN_H // 2, 0), rows - WIN_H) for ri in range(rq)]
        w0 = min(r0[0], rows - kr)
        key = (w0 - first,) + tuple(a - w0 for a in r0)
        if key not in keys:
            keys.append(key)
        w0s.append(w0)
        var.append(keys.index(key))
    return keys, np.asarray(w0s, np.int32), np.asarray(var, np.int32)


def _bias_tables(rpb, keys, rq):
    heads = rpb.shape[0]
    kr = WIN_H + rq - 1
    col = jnp.arange(GRID_W)
    col_start = jnp.clip(col - WIN_W // 2, 0, GRID_W - WIN_W)
    valid = (col[None, :] >= col_start[:, None]) & (col[None, :] < col_start[:, None] + WIN_W)
    col_off = jnp.clip(col[None, :] - col[:, None], -(WIN_W - 1), WIN_W - 1) + (WIN_W - 1)
    onehot = (col_off[None] == jnp.arange(2 * WIN_W - 1)[:, None, None]).astype(F32)
    tb = jnp.einsum('hrc,cqk->hqrk', rpb, onehot, precision=lax.Precision.HIGHEST)
    tb = jnp.where(valid[None, :, None, :], tb, NEG_INF)
    tb = jnp.pad(tb, ((0, 0), (0, 0), (kr, kr), (0, 0)), constant_values=NEG_INF)
    out = []
    for key in keys:
        off, starts = key[0], key[1:]
        per_row = []
        for ri in range(rq):
            krs = np.arange(kr)
            in_win = (krs >= starts[ri]) & (krs < starts[ri] + WIN_H)
            first = off - ri + WIN_H - 1 + kr
            t = jnp.where(jnp.asarray(in_win)[None, None, :, None], tb[:, :, first:first + kr], NEG_INF)
            per_row.append(t.reshape(heads, GRID_W, kr * GRID_W))
        t = jnp.stack(per_row, axis=1)
        out.append(t.reshape(heads // 2, 2 * rq * GRID_W, kr * GRID_W))
    return jnp.stack(out)


def _na_attention(qkv, qkv_ctx, rpb, rq):
    b, seq, d3 = qkv.shape
    d = d3 // 3
    rows = seq // GRID_W
    lc = qkv_ctx.shape[1]
    kr = WIN_H + rq - 1
    nq = rq * GRID_W
    keys, w0s, var = _na_plan(rows, rq)
    bias = _bias_tables(rpb, keys, rq)
    win = (pl.Element(1), pl.Element(kr * GRID_W), pl.Element(d))
    grid_spec = pltpu.PrefetchScalarGridSpec(
        num_scalar_prefetch=2,
        grid=(b, rows // rq),
        in_specs=[
            pl.BlockSpec((None, nq, d), lambda i, j, w0, vr: (i, j, 0)),
            pl.BlockSpec(win, lambda i, j, w0, vr: (i, w0[j] * GRID_W, d)),
            pl.BlockSpec(win, lambda i, j, w0, vr: (i, w0[j] * GRID_W, 2 * d)),
            pl.BlockSpec((None, lc, d), lambda i, j, w0, vr: (i, 0, 1)),
            pl.BlockSpec((None, lc, d), lambda i, j, w0, vr: (i, 0, 2)),
            pl.BlockSpec((None,) + bias.shape[1:], lambda i, j, w0, vr: (vr[j], 0, 0, 0),
                         pipeline_mode=pl.Buffered(1)),
        ],
        out_specs=pl.BlockSpec((None, nq, d), lambda i, j, w0, vr: (i, j, 0)),
    )
    return pl.pallas_call(
        _na_kernel,
        grid_spec=grid_spec,
        out_shape=jax.ShapeDtypeStruct((b, seq, d), BF16),
        compiler_params=_params("parallel", "parallel"),
        name="na_attention",
    )(jnp.asarray(w0s), jnp.asarray(var), qkv, qkv, qkv, qkv_ctx, qkv_ctx, bias)


def _ctx_attn_kernel(q_ref, k_ref, v_ref, o_ref):
    nq, d = q_ref.shape
    lo_mask = lax.broadcasted_iota(jnp.int32, (nq, LANES), 1) < NA_HEAD_DIM
    for p in range(d // LANES):
        cols = slice(p * LANES, (p + 1) * LANES)
        qq = _head_pair_queries(q_ref, cols, lo_mask)
        s = _nt_dot(qq, k_ref[:, cols])
        e = jnp.exp(s - jnp.max(s, axis=-1, keepdims=True))
        r = jnp.dot(e.astype(BF16), v_ref[:, cols], preferred_element_type=F32) / jnp.sum(e, axis=-1, keepdims=True)
        o_ref[:, cols] = jnp.where(lo_mask, r[:nq], r[nq:]).astype(o_ref.dtype)


def _ctx_attention(qkv_ctx):
    b, lc, d3 = qkv_ctx.shape
    d = d3 // 3
    return pl.pallas_call(
        _ctx_attn_kernel,
        grid=(b,),
        in_specs=[pl.BlockSpec((None, lc, d), lambda i, j=j: (i, 0, j)) for j in range(3)],
        out_specs=pl.BlockSpec((None, lc, d), lambda i: (i, 0, 0)),
        out_shape=jax.ShapeDtypeStruct((b, lc, d), BF16),
        compiler_params=_params("parallel"),
        name="ctx_attention",
    )(qkv_ctx, qkv_ctx, qkv_ctx)


def _oproj_kernel(a_ref, x_ref, mod_ref, w_ref, o_ref):
    y = jnp.dot(a_ref[...], w_ref[...], preferred_element_type=F32)
    o_ref[...] = x_ref[...] + mod_ref[2:3, :] * y


def _out_proj(a, x, mod, w, tm):
    b, seq, d = x.shape
    return pl.pallas_call(
        _oproj_kernel,
        grid=(b, seq // tm),
        in_specs=[
            pl.BlockSpec((None, tm, d), lambda i, t: (i, t, 0)),
            pl.BlockSpec((None, tm, d), lambda i, t: (i, t, 0)),
            pl.BlockSpec((None, N_MOD, d), lambda i, t: (i, 0, 0)),
            pl.BlockSpec((d, d), lambda i, t: (0, 0)),
        ],
        out_specs=pl.BlockSpec((None, tm, d), lambda i, t: (i, t, 0)),
        out_shape=jax.ShapeDtypeStruct(x.shape, F32),
        compiler_params=_params("parallel", "parallel"),
        name="attn_out_proj",
    )(a, x, mod, w)


def _glu_kernel(x_ref, mod_ref, wa_ref, wg_ref, ba_ref, bg_ref, o_ref, h_ref):
    @pl.when(pl.program_id(2) == 0)
    def _():
        h_ref[...] = _ada(x_ref[...], mod_ref[0:1, :], mod_ref[1:2, :]).astype(BF16)

    h = h_ref[...]
    a = jnp.dot(h, wa_ref[...], preferred_element_type=F32) + ba_ref[...]
    g = jnp.dot(h, wg_ref[...], preferred_element_type=F32) + bg_ref[...]
    o_ref[...] = a * jax.nn.sigmoid(g)


def _glu_proj(x, mod, w, bias, tm, tn):
    b, seq, d = x.shape
    nn = d // tn
    return pl.pallas_call(
        _glu_kernel,
        grid=(b, seq // tm, nn),
        in_specs=[
            pl.BlockSpec((None, tm, d), lambda i, t, j: (i, t, 0)),
            pl.BlockSpec((None, N_MOD, d), lambda i, t, j: (i, 0, 0)),
            pl.BlockSpec((d, tn), lambda i, t, j: (0, j)),
            pl.BlockSpec((d, tn), lambda i, t, j: (0, nn + j)),
            pl.BlockSpec((1, tn), lambda i, t, j: (0, j)),
            pl.BlockSpec((1, tn), lambda i, t, j: (0, nn + j)),
        ],
        out_specs=pl.BlockSpec((None, tm, tn), lambda i, t, j: (i, t, j)),
        out_shape=jax.ShapeDtypeStruct((b, seq, d), F32),
        scratch_shapes=[pltpu.VMEM((tm, d), BF16)],
        compiler_params=_params("parallel", "parallel", "arbitrary"),
        name="conv_glu_proj",
    )(x, mod, w, w, bias, bias)


def _convmod_kernel(u_ref, up_ref, un_ref, x_ref, mod_ref, dw_ref, bdw_ref, lng_ref, lnb_ref,
                    w2_ref, b2_ref, o_ref, us_ref, ys_ref, *, tm, nt):
    t = pl.program_id(1)
    nc = us_ref.shape[0]
    width = dw_ref.shape[1]
    half = width // 2
    sub = 8 * CONV_NQ

    def column_block(c, carry):
        cs = pl.ds(pl.multiple_of(c * LANES, LANES), LANES)
        us_ref[c, HALO:HALO + tm, :] = u_ref[:, cs]
        us_ref[c, 0:HALO, :] = jnp.where(t > 0, up_ref[:, cs], 0.0)
        us_ref[c, HALO + tm:, :] = jnp.where(t < nt - 1, un_ref[:, cs], 0.0)
        for m in range(tm // sub):
            e = jnp.concatenate([us_ref[c, pl.ds(HALO + m * sub - half + j, 8, stride=CONV_NQ), :]
                                 for j in range(CONV_NQ + width - 1)], axis=0)
            acc = bdw_ref[c] + dw_ref[c, 0:1, :] * e[0:sub]
            for k in range(1, width):
                acc = acc + dw_ref[c, k:k + 1, :] * e[8 * k:8 * k + sub]
            for q in range(CONV_NQ):
                ys_ref[c, pl.ds(m * sub + q, 8, stride=CONV_NQ), :] = acc[8 * q:8 * q + 8]
        return carry

    lax.fori_loop(0, nc, column_block, 0)
    acc = jnp.concatenate([ys_ref[c] for c in range(nc)], axis=1)
    mu = jnp.mean(acc, axis=-1, keepdims=True)
    cen = acc - mu
    var = jnp.mean(cen * cen, axis=-1, keepdims=True)
    z = jax.nn.silu(cen * lax.rsqrt(var + LN_EPS) * lng_ref[...] + lnb_ref[...])
    y = jnp.dot(z.astype(BF16), w2_ref[...], preferred_element_type=F32) + b2_ref[...]
    o_ref[...] = x_ref[...] + mod_ref[2:3, :] * y


def _conv_module_tail(u, x, mod, w_dw, b_dw, ln_g, ln_b, w2, b2, tm):
    b, seq, d = x.shape
    nt = seq // tm
    nc = d // LANES
    width = w_dw.shape[0]
    assert width // 2 < HALO and tm % (8 * CONV_NQ) == 0
    prev, nxt = _halo_maps(tm, seq)
    row = lambda i, t: (0, 0)
    blocked = lambda i, t: (0, 0, 0)
    w_dw = w_dw.reshape(width, nc, LANES).transpose(1, 0, 2)
    b_dw = b_dw.reshape(nc, 1, LANES)
    return pl.pallas_call(
        functools.partial(_convmod_kernel, tm=tm, nt=nt),
        grid=(b, nt),
        in_specs=[
            pl.BlockSpec((None, tm, d), lambda i, t: (i, t, 0)),
            pl.BlockSpec((None, HALO, d), prev),
            pl.BlockSpec((None, HALO, d), nxt),
            pl.BlockSpec((None, tm, d), lambda i, t: (i, t, 0)),
            pl.BlockSpec((None, N_MOD, d), lambda i, t: (i, 0, 0)),
            pl.BlockSpec((nc, width, LANES), blocked),
            pl.BlockSpec((nc, 1, LANES), blocked),
            pl.BlockSpec((1, d), row),
            pl.BlockSpec((1, d), row),
            pl.BlockSpec((d, d), row),
            pl.BlockSpec((1, d), row),
        ],
        out_specs=pl.BlockSpec((None, tm, d), lambda i, t: (i, t, 0)),
        out_shape=jax.ShapeDtypeStruct(x.shape, F32),
        scratch_shapes=[pltpu.VMEM((nc, tm + 2 * HALO, LANES), F32), pltpu.VMEM((nc, tm, LANES), F32)],
        compiler_params=_params("parallel", "parallel"),
        name="conv_module_tail",
    )(u, u, u, x, mod, w_dw, b_dw, ln_g, ln_b, w2, b2)


def kernel(x, c, ctx, c_ctx, w_mod, b_mod, pool_w, pool_scale, na_w_qkv, na_w_o, na_rpb, cv_w_pw1, cv_b_pw1, cv_w_dw, cv_b_dw, cv_ln_g, cv_ln_b, cv_w_pw2, cv_b_pw2, ffn_w_up, ffn_w_dw, ffn_b_dw, ffn_w_down, final_norm_g):
    batch, seq, d = x.shape
    depth = w_mod.shape[0]
    lc = ctx.shape[1]
    assert seq % GRID_W == 0 and seq // GRID_W >= WIN_H and d % (2 * LANES) == 0
    tm = min(512, seq)
    tmm = min(1024, seq)
    tmc = lc
    fc = 512

    c_all = jnp.concatenate([c, c_ctx[None]], axis=0)
    m_all = _modulation(c_all, w_mod, b_mod).reshape(depth, 8, N_MOD, d)
    row = lambda v: v.reshape(1, -1)

    for i in range(depth):
        kind, j = i % N_MIXERS, i // N_MIXERS
        update_ctx = i < depth - 1
        mod = m_all[i, :batch]
        mod_c = jnp.broadcast_to(m_all[i, batch:batch + 1], (batch, N_MOD, d))
        if kind == 0:
            w_grp = pool_w[j].astype(BF16)
            x_mid = _pool_mixer(x, mod, w_grp, row(pool_scale[j]), tm)
            if update_ctx:
                ctx_mid = _pool_mixer(ctx, mod_c, w_grp, row(pool_scale[j]), tmc)
        elif kind == 1:
            w_qkv = na_w_qkv[j].astype(BF16)
            w_o = na_w_o[j].astype(BF16)
            qkv = _qkv_proj(x, mod, w_qkv, tmm, 1024)
            qkv_c = _qkv_proj(ctx, mod_c, w_qkv, tmc, 1024)
            attn = _na_attention(qkv, qkv_c, na_rpb[j], NA_ROWS)
            x_mid = _out_proj(attn, x, mod, w_o, tm)
            if update_ctx:
                ctx_mid = _out_proj(_ctx_attention(qkv_c), ctx, mod_c, w_o, tmc)
        else:
            w1 = cv_w_pw1[j].astype(BF16)
            w2 = cv_w_pw2[j].astype(BF16)
            tail = (cv_w_dw[j], row(cv_b_dw[j]), row(cv_ln_g[j]), row(cv_ln_b[j]), w2, row(cv_b_pw2[j]))
            u = _glu_proj(x, mod, w1, row(cv_b_pw1[j]), tmm, 512)
            x_mid = _conv_module_tail(u, x, mod, *tail, tm=256)
            if update_ctx:
                u_c = _glu_proj(ctx, mod_c, w1, row(cv_b_pw1[j]), tmc, 512)
                ctx_mid = _conv_module_tail(u_c, ctx, mod_c, *tail, tm=tmc)
        ffn = (ffn_w_up[i].astype(BF16), ffn_w_dw[i], row(ffn_b_dw[i]), ffn_w_down[i].astype(BF16), row(final_norm_g))
        x = _conv_ffn(x_mid, mod, *ffn, tm=tm, fc=fc, final=(i == depth - 1))
        if update_ctx:
            ctx = _conv_ffn(ctx_mid, mod_c, *ffn, tm=tmc, fc=fc, final=False)
    return x
```

```python
import functools

import numpy as np
import jax
import jax.numpy as jnp
from jax import lax
from jax.experimental import pallas as pl
from jax.experimental.pallas import tpu as pltpu

F32 = jnp.float32
BF16 = jnp.bfloat16

N_MIXERS = 3
N_MOD = 6
POOL_WINDOWS = (2, 4, 8, 16)
GRID_W = 64
NA_HEAD_DIM = 64
WIN_H = 8
WIN_W = 16
NORM_EPS = 1e-6
LN_EPS = 1e-5
NEG_INF = -1e30

LANES = 128
HALO = 16
CONV_NQ = 8
NA_ROWS = 4
VMEM_LIMIT = 58 * 1024 * 1024


def _params(*sem):
    return pltpu.CompilerParams(dimension_semantics=sem, vmem_limit_bytes=VMEM_LIMIT)


def _rms(xf):
    return xf * lax.rsqrt(jnp.mean(xf * xf, axis=-1, keepdims=True) + NORM_EPS)


def _ada(xf, shift, scale):
    return _rms(xf) * (1.0 + scale) + shift


def _halo_maps(tm, seq):
    per = tm // HALO
    last = seq // HALO - 1
    prev = lambda b, t, *_: (b, jnp.maximum(t * per - 1, 0), 0)
    nxt = lambda b, t, *_: (b, jnp.minimum((t + 1) * per, last), 0)
    return prev, nxt


def _mod_kernel(cb_ref, w_ref, b_ref, o_ref):
    w = w_ref[...]
    reps = w.shape[1] // LANES
    n_rows = cb_ref.shape[0]
    for r in range(n_rows):
        cb = jax.nn.silu(cb_ref[r])
        o_ref[r:r + 1, :] = jnp.sum(w * jnp.tile(cb, (1, reps)), axis=0, keepdims=True) + b_ref[...]
    o_ref[n_rows:, :] = jnp.zeros((o_ref.shape[0] - n_rows, w.shape[1]), F32)


def _modulation(c_all, w_mod, b_mod, tn=2048):
    depth, d, n = w_mod.shape
    rows = c_all.shape[0]
    cb = jnp.broadcast_to(c_all[:, :, None], (rows, d, LANES))
    return pl.pallas_call(
        _mod_kernel,
        grid=(depth, n // tn),
        in_specs=[
            pl.BlockSpec((rows, d, LANES), lambda l, j: (0, 0, 0)),
            pl.BlockSpec((None, d, tn), lambda l, j: (l, 0, j)),
            pl.BlockSpec((None, 1, tn), lambda l, j: (l, 0, j)),
        ],
        out_specs=pl.BlockSpec((None, 8, tn), lambda l, j: (l, 0, j)),
        out_shape=jax.ShapeDtypeStruct((depth, 8, n), F32),
        compiler_params=_params("parallel", "parallel"),
        name="modulation",
    )(cb, w_mod, b_mod.reshape(depth, 1, n))


def _pool_kernel(x_ref, xp_ref, xn_ref, mod_ref, w_ref, ls_ref, o_ref, hs_ref, *, tm, nt, seq):
    t = pl.program_id(1)
    shift, scale, gate = mod_ref[0:1, :], mod_ref[1:2, :], mod_ref[2:3, :]
    x = x_ref[...]
    hs_ref[HALO:HALO + tm, :] = _ada(x, shift, scale)
    hs_ref[0:HALO, :] = jnp.where(t > 0, _ada(xp_ref[...], shift, scale), 0.0)
    hs_ref[HALO + tm:, :] = jnp.where(t < nt - 1, _ada(xn_ref[...], shift, scale), 0.0)
    grp = x.shape[1] // len(POOL_WINDOWS)
    pos = t * tm + lax.broadcasted_iota(jnp.int32, (tm, grp), 0)
    for g, w in enumerate(POOL_WINDOWS):
        cols = slice(g * grp, (g + 1) * grp)
        start = HALO - w // 2
        s = hs_ref[pl.ds(start, tm), cols]
        for k in range(1, w):
            s = s + hs_ref[pl.ds(start + k, tm), cols]
        cnt = (jnp.clip(pos - w // 2 + w, 0, seq) - jnp.clip(pos - w // 2, 0, seq)).astype(F32)
        d = (s / cnt - hs_ref[HALO:HALO + tm, cols]).astype(BF16)
        y = jnp.dot(d, w_ref[g], preferred_element_type=F32)
        o_ref[:, cols] = x[:, cols] + gate[:, cols] * (y * ls_ref[:, cols])


def _pool_mixer(x, mod, w_grp, ls, tm):
    b, seq, d = x.shape
    nt = seq // tm
    prev, nxt = _halo_maps(tm, seq)
    return pl.pallas_call(
        functools.partial(_pool_kernel, tm=tm, nt=nt, seq=seq),
        grid=(b, nt),
        in_specs=[
            pl.BlockSpec((None, tm, d), lambda i, t: (i, t, 0)),
            pl.BlockSpec((None, HALO, d), prev),
            pl.BlockSpec((None, HALO, d), nxt),
            pl.BlockSpec((None, N_MOD, d), lambda i, t: (i, 0, 0)),
            pl.BlockSpec(w_grp.shape, lambda i, t: (0, 0, 0)),
            pl.BlockSpec((1, d), lambda i, t: (0, 0)),
        ],
        out_specs=pl.BlockSpec((None, tm, d), lambda i, t: (i, t, 0)),
        out_shape=jax.ShapeDtypeStruct(x.shape, F32),
        scratch_shapes=[pltpu.VMEM((tm + 2 * HALO, d), F32)],
        compiler_params=_params("parallel", "parallel"),
        name="pool_mixer",
    )(x, x, x, mod, w_grp, ls)


def _ffn_kernel(x_ref, xp_ref, xn_ref, mod_ref, wg_ref, wv_ref, dwg_ref, dwv_ref, bg_ref, bv_ref,
                wd_ref, fg_ref, o_ref, h_ref, ug_ref, uv_ref, *, tm, nt, final):
    t = pl.program_id(1)
    f = pl.program_id(2)
    shift, scale, gate = mod_ref[3:4, :], mod_ref[4:5, :], mod_ref[5:6, :]

    @pl.when(f == 0)
    def _():
        h_ref[HALO:HALO + tm, :] = _ada(x_ref[...], shift, scale).astype(BF16)
        h_ref[0:HALO, :] = jnp.where(t > 0, _ada(xp_ref[...], shift, scale), 0.0).astype(BF16)
        h_ref[HALO + tm:, :] = jnp.where(t < nt - 1, _ada(xn_ref[...], shift, scale), 0.0).astype(BF16)
        o_ref[...] = jnp.zeros(o_ref.shape, F32)

    h = h_ref[...]
    ug_ref[...] = jnp.dot(h, wg_ref[...], preferred_element_type=F32)
    uv_ref[...] = jnp.dot(h, wv_ref[...], preferred_element_type=F32)

    def conv3(u_ref, dw_ref, b_ref):
        return (dw_ref[0:1, :] * u_ref[pl.ds(HALO - 1, tm), :]
                + dw_ref[1:2, :] * u_ref[pl.ds(HALO, tm), :]
                + dw_ref[2:3, :] * u_ref[pl.ds(HALO + 1, tm), :]
                + b_ref[...])

    act = jax.nn.silu(conv3(ug_ref, dwg_ref, bg_ref)) * conv3(uv_ref, dwv_ref, bv_ref)
    o_ref[...] += jnp.dot(act.astype(BF16), wd_ref[...], preferred_element_type=F32)

    @pl.when(f == pl.num_programs(2) - 1)
    def _():
        xn = x_ref[...] + gate * o_ref[...]
        if final:
            xn = _rms(xn) * fg_ref[...]
        o_ref[...] = xn


def _conv_ffn(x, mod, w_up, w_dw, b_dw, w_down, final_g, tm, fc, final):
    b, seq, d = x.shape
    dff = w_down.shape[0]
    nf = dff // fc
    nt = seq // tm
    prev, nxt = _halo_maps(tm, seq)
    x_mode = dict(pipeline_mode=pl.Buffered(1)) if 4 * tm * d * 4 > VMEM_LIMIT // 2 else {}
    return pl.pallas_call(
        functools.partial(_ffn_kernel, tm=tm, nt=nt, final=final),
        grid=(b, nt, nf),
        in_specs=[
            pl.BlockSpec((None, tm, d), lambda i, t, f: (i, t, 0), **x_mode),
            pl.BlockSpec((None, HALO, d), prev),
            pl.BlockSpec((None, HALO, d), nxt),
            pl.BlockSpec((None, N_MOD, d), lambda i, t, f: (i, 0, 0)),
            pl.BlockSpec((d, fc), lambda i, t, f: (0, f)),
            pl.BlockSpec((d, fc), lambda i, t, f: (0, nf + f)),
            pl.BlockSpec((w_dw.shape[0], fc), lambda i, t, f: (0, f)),
            pl.BlockSpec((w_dw.shape[0], fc), lambda i, t, f: (0, nf + f)),
            pl.BlockSpec((1, fc), lambda i, t, f: (0, f)),
            pl.BlockSpec((1, fc), lambda i, t, f: (0, nf + f)),
            pl.BlockSpec((fc, d), lambda i, t, f: (f, 0)),
            pl.BlockSpec((1, d), lambda i, t, f: (0, 0)),
        ],
        out_specs=pl.BlockSpec((None, tm, d), lambda i, t, f: (i, t, 0)),
        out_shape=jax.ShapeDtypeStruct(x.shape, F32),
        scratch_shapes=[
            pltpu.VMEM((tm + 2 * HALO, d), BF16),
            pltpu.VMEM((tm + 2 * HALO, fc), F32),
            pltpu.VMEM((tm + 2 * HALO, fc), F32),
        ],
        compiler_params=_params("parallel", "parallel", "arbitrary"),
        name="conv_ffn",
    )(x, x, x, mod, w_up, w_up, w_dw, w_dw, b_dw, b_dw, w_down, final_g)


def _qkv_kernel(x_ref, mod_ref, w_ref, o_ref, h_ref):
    @pl.when(pl.program_id(2) == 0)
    def _():
        h_ref[...] = _ada(x_ref[...], mod_ref[0:1, :], mod_ref[1:2, :]).astype(BF16)

    o_ref[...] = jnp.dot(h_ref[...], w_ref[...], preferred_element_type=F32).astype(o_ref.dtype)


def _qkv_proj(x, mod, w, tm, tn):
    b, seq, d = x.shape
    n = w.shape[1]
    return pl.pallas_call(
        _qkv_kernel,
        grid=(b, seq // tm, n // tn),
        in_specs=[
            pl.BlockSpec((None, tm, d), lambda i, t, j: (i, t, 0)),
            pl.BlockSpec((None, N_MOD, d), lambda i, t, j: (i, 0, 0)),
            pl.BlockSpec((d, tn), lambda i, t, j: (0, j)),
        ],
        out_specs=pl.BlockSpec((None, tm, tn), lambda i, t, j: (i, t, j)),
        out_shape=jax.ShapeDtypeStruct((b, seq, n), BF16),
        scratch_shapes=[pltpu.VMEM((tm, d), BF16)],
        compiler_params=_params("parallel", "parallel", "arbitrary"),
        name="qkv_proj",
    )(x, mod, w)


def _head_pair_queries(q_ref, cols, lo_mask):
    q2 = q_ref[:, cols].astype(F32) * (NA_HEAD_DIM ** -0.5)
    return jnp.concatenate([jnp.where(lo_mask, q2, 0.0), jnp.where(lo_mask, 0.0, q2)], axis=0).astype(BF16)


def _nt_dot(a, b):
    return lax.dot_general(a, b, (((1,), (1,)), ((), ())), preferred_element_type=F32)


def _na_kernel(w0_ref, var_ref, q_ref, k_ref, v_ref, kc_ref, vc_ref, bias_ref, o_ref):
    nq, d = q_ref.shape
    lo_mask = lax.broadcasted_iota(jnp.int32, (nq, LANES), 1) < NA_HEAD_DIM
    ones_loc = jnp.ones((k_ref.shape[1], LANES), BF16)
    ones_ctx = jnp.ones((kc_ref.shape[0], LANES), BF16)
    for p in range(d // LANES):
        cols = slice(p * LANES, (p + 1) * LANES)
        qq = _head_pair_queries(q_ref, cols, lo_mask)
        bias = jnp.concatenate([bias_ref[ri, p, hd] for hd in range(2) for ri in range(bias_ref.shape[0])], axis=0)
        s_loc = _nt_dot(qq, k_ref[0, :, cols]) + bias
        s_ctx = _nt_dot(qq, kc_ref[:, cols])
        m = jnp.maximum(jnp.max(s_loc, axis=-1, keepdims=True), jnp.max(s_ctx, axis=-1, keepdims=True))
        p_loc = jnp.exp(s_loc - m).astype(BF16)
        p_ctx = jnp.exp(s_ctx - m).astype(BF16)
        v2 = jnp.concatenate([v_ref[0, :, cols], ones_loc], axis=1)
        vc2 = jnp.concatenate([vc_ref[:, cols], ones_ctx], axis=1)
        r2 = (jnp.dot(p_loc, v2, preferred_element_type=F32) + jnp.dot(p_ctx, vc2, preferred_element_type=F32))
        r = r2[:, :LANES] / r2[:, LANES:LANES + 1]
        o_ref[:, cols] = jnp.where(lo_mask, r[:nq], r[nq:]).astype(o_ref.dtype)


def _na_plan(rows, rq):
    kr = WIN_H + rq - 1
    keys, w0s, var = [], [], []
    for j in range(rows // rq):
        first = j * rq
        r0 = [min(max(first + ri - WIN_H // 2, 0), rows - WIN_H) for ri in range(rq)]
        w0 = min(r0[0], rows - kr)
        key = (w0 - first,) + tuple(a - w0 for a in r0)
        if key not in keys:
            keys.append(key)
        w0s.append(w0)
        var.append(keys.index(key))
    return keys, np.asarray(w0s, np.int32), np.asarray(var, np.int32)


def _bias_tables(rpb, keys, rq):
    heads = rpb.shape[0]
    kr = WIN_H + rq - 1
    n_off = 2 * WIN_H - 1
    col = jnp.arange(GRID_W)
    col_start = jnp.clip(col - WIN_W // 2, 0, GRID_W - WIN_W)
    valid = (col[None, :] >= col_start[:, None]) & (col[None, :] < col_start[:, None] + WIN_W)
    col_off = jnp.clip(col[None, :] - col[:, None], -(WIN_W - 1), WIN_W - 1) + (WIN_W - 1)
    onehot = (col_off[None] == jnp.arange(2 * WIN_W - 1)[:, None, None]).astype(F32)
    tb = jnp.einsum('hrc,cqk->hqrk', rpb, onehot, precision=lax.Precision.HIGHEST)
    tb = jnp.where(valid[None, :, None, :], tb, NEG_INF).reshape(heads, GRID_W, n_off * GRID_W)
    tb = jnp.pad(tb, ((0, 0), (0, 0), (kr * GRID_W, (kr + 1) * GRID_W)), constant_values=NEG_INF)
    out = []
    for key in keys:
        off, starts = key[0], key[1:]
        per_row = []
        for ri in range(rq):
            krs = np.repeat(np.arange(kr), GRID_W)
            in_win = (krs >= starts[ri]) & (krs < starts[ri] + WIN_H)
            first = (off - ri + WIN_H - 1 + kr) * GRID_W
            per_row.append(jnp.where(jnp.asarray(in_win)[None, None, :], tb[:, :, first:first + kr * GRID_W], NEG_INF))
        out.append(jnp.stack(per_row).reshape(rq, heads // 2, 2, GRID_W, kr * GRID_W))
    return jnp.stack(out)


def _na_attention(qkv, qkv_ctx, rpb, rq):
    b, seq, d3 = qkv.shape
    d = d3 // 3
    rows = seq // GRID_W
    lc = qkv_ctx.shape[1]
    kr = WIN_H + rq - 1
    nq = rq * GRID_W
    keys, w0s, var = _na_plan(rows, rq)
    bias = _bias_tables(rpb, keys, rq)
    win = (pl.Element(1), pl.Element(kr * GRID_W), pl.Element(d))
    grid_spec = pltpu.PrefetchScalarGridSpec(
        num_scalar_prefetch=2,
        grid=(b, rows // rq),
        in_specs=[
            pl.BlockSpec((None, nq, d), lambda i, j, w0, vr: (i, j, 0)),
            pl.BlockSpec(win, lambda i, j, w0, vr: (i, w0[j] * GRID_W, d)),
            pl.BlockSpec(win, lambda i, j, w0, vr: (i, w0[j] * GRID_W, 2 * d)),
            pl.BlockSpec((None, lc, d), lambda i, j, w0, vr: (i, 0, 1)),
            pl.BlockSpec((None, lc, d), lambda i, j, w0, vr: (i, 0, 2)),
            pl.BlockSpec((None,) + bias.shape[1:], lambda i, j, w0, vr: (vr[j], 0, 0, 0, 0, 0),
                         pipeline_mode=pl.Buffered(1)),
        ],
        out_specs=pl.BlockSpec((None, nq, d), lambda i, j, w0, vr: (i, j, 0)),
    )
    return pl.pallas_call(
        _na_kernel,
        grid_spec=grid_spec,
        out_shape=jax.ShapeDtypeStruct((b, seq, d), BF16),
        compiler_params=_params("parallel", "parallel"),
        name="na_attention",
    )(jnp.asarray(w0s), jnp.asarray(var), qkv, qkv, qkv, qkv_ctx, qkv_ctx, bias)


def _ctx_attn_kernel(q_ref, k_ref, v_ref, o_ref):
    nq, d = q_ref.shape
    lo_mask = lax.broadcasted_iota(jnp.int32, (nq, LANES), 1) < NA_HEAD_DIM
    for p in range(d // LANES):
        cols = slice(p * LANES, (p + 1) * LANES)
        qq = _head_pair_queries(q_ref, cols, lo_mask)
        s = _nt_dot(qq, k_ref[:, cols])
        e = jnp.exp(s - jnp.max(s, axis=-1, keepdims=True))
        r = jnp.dot(e.astype(BF16), v_ref[:, cols], preferred_element_type=F32) / jnp.sum(e, axis=-1, keepdims=True)
        o_ref[:, cols] = jnp.where(lo_mask, r[:nq], r[nq:]).astype(o_ref.dtype)


def _ctx_attention(qkv_ctx):
    b, lc, d3 = qkv_ctx.shape
    d = d3 // 3
    return pl.pallas_call(
        _ctx_attn_kernel,
        grid=(b,),
        in_specs=[pl.BlockSpec((None, lc, d), lambda i, j=j: (i, 0, j)) for j in range(3)],
        out_specs=pl.BlockSpec((None, lc, d), lambda i: (i, 0, 0)),
        out_shape=jax.ShapeDtypeStruct((b, lc, d), BF16),
        compiler_params=_params("parallel"),
        name="ctx_attention",
    )(qkv_ctx, qkv_ctx, qkv_ctx)


def _oproj_kernel(a_ref, x_ref, mod_ref, w_ref, o_ref):
    y = jnp.dot(a_ref[...], w_ref[...], preferred_element_type=F32)
    o_ref[...] = x_ref[...] + mod_ref[2:3, :] * y


def _out_proj(a, x, mod, w, tm):
    b, seq, d = x.shape
    return pl.pallas_call(
        _oproj_kernel,
        grid=(b, seq // tm),
        in_specs=[
            pl.BlockSpec((None, tm, d), lambda i, t: (i, t, 0)),
            pl.BlockSpec((None, tm, d), lambda i, t: (i, t, 0)),
            pl.BlockSpec((None, N_MOD, d), lambda i, t: (i, 0, 0)),
            pl.BlockSpec((d, d), lambda i, t: (0, 0)),
        ],
        out_specs=pl.BlockSpec((None, tm, d), lambda i, t: (i, t, 0)),
        out_shape=jax.ShapeDtypeStruct(x.shape, F32),
        compiler_params=_params("parallel", "parallel"),
        name="attn_out_proj",
    )(a, x, mod, w)


def _glu_kernel(x_ref, mod_ref, wa_ref, wg_ref, ba_ref, bg_ref, o_ref, h_ref):
    @pl.when(pl.program_id(2) == 0)
    def _():
        h_ref[...] = _ada(x_ref[...], mod_ref[0:1, :], mod_ref[1:2, :]).astype(BF16)

    h = h_ref[...]
    a = jnp.dot(h, wa_ref[...], preferred_element_type=F32) + ba_ref[...]
    g = jnp.dot(h, wg_ref[...], preferred_element_type=F32) + bg_ref[...]
    o_ref[...] = a * jax.nn.sigmoid(g)


def _glu_proj(x, mod, w, bias, tm, tn):
    b, seq, d = x.shape
    nn = d // tn
    return pl.pallas_call(
        _glu_kernel,
        grid=(b, seq // tm, nn),
        in_specs=[
            pl.BlockSpec((None, tm, d), lambda i, t, j: (i, t, 0)),
            pl.BlockSpec((None, N_MOD, d), lambda i, t, j: (i, 0, 0)),
            pl.BlockSpec((d, tn), lambda i, t, j: (0, j)),
            pl.BlockSpec((d, tn), lambda i, t, j: (0, nn + j)),
            pl.BlockSpec((1, tn), lambda i, t, j: (0, j)),
            pl.BlockSpec((1, tn), lambda i, t, j: (0, nn + j)),
        ],
        out_specs=pl.BlockSpec((None, tm, tn), lambda i, t, j: (i, t, j)),
        out_shape=jax.ShapeDtypeStruct((b, seq, d), F32),
        scratch_shapes=[pltpu.VMEM((tm, d), BF16)],
        compiler_params=_params("parallel", "parallel", "arbitrary"),
        name="conv_glu_proj",
    )(x, mod, w, w, bias, bias)


def _convmod_kernel(u_ref, up_ref, un_ref, x_ref, mod_ref, dw_ref, bdw_ref, lng_ref, lnb_ref,
                    w2_ref, b2_ref, o_ref, us_ref, ys_ref, *, tm, nt):
    t = pl.program_id(1)
    nc = us_ref.shape[0]
    width = dw_ref.shape[1]
    half = width // 2
    sub = 8 * CONV_NQ

    def column_block(c, carry):
        cs = pl.ds(pl.multiple_of(c * LANES, LANES), LANES)
        us_ref[c, HALO:HALO + tm, :] = u_ref[:, cs]
        us_ref[c, 0:HALO, :] = jnp.where(t > 0, up_ref[:, cs], 0.0)
        us_ref[c, HALO + tm:, :] = jnp.where(t < nt - 1, un_ref[:, cs], 0.0)
        for m in range(tm // sub):
            e = jnp.concatenate([us_ref[c, pl.ds(HALO + m * sub - half + j, 8, stride=CONV_NQ), :]
                                 for j in range(CONV_NQ + width - 1)], axis=0)
            acc = bdw_ref[c] + dw_ref[c, 0:1, :] * e[0:sub]
            for k in range(1, width):
                acc = acc + dw_ref[c, k:k + 1, :] * e[8 * k:8 * k + sub]
            for q in range(CONV_NQ):
                ys_ref[c, pl.ds(m * sub + q, 8, stride=CONV_NQ), :] = acc[8 * q:8 * q + 8]
        return carry

    lax.fori_loop(0, nc, column_block, 0)
    acc = jnp.concatenate([ys_ref[c] for c in range(nc)], axis=1)
    mu = jnp.mean(acc, axis=-1, keepdims=True)
    cen = acc - mu
    var = jnp.mean(cen * cen, axis=-1, keepdims=True)
    z = jax.nn.silu(cen * lax.rsqrt(var + LN_EPS) * lng_ref[...] + lnb_ref[...])
    y = jnp.dot(z.astype(BF16), w2_ref[...], preferred_element_type=F32) + b2_ref[...]
    o_ref[...] = x_ref[...] + mod_ref[2:3, :] * y


def _conv_module_tail(u, x, mod, w_dw, b_dw, ln_g, ln_b, w2, b2, tm):
    b, seq, d = x.shape
    nt = seq // tm
    nc = d // LANES
    width = w_dw.shape[0]
    assert width // 2 < HALO and tm % (8 * CONV_NQ) == 0
    prev, nxt = _halo_maps(tm, seq)
    row = lambda i, t: (0, 0)
    blocked = lambda i, t: (0, 0, 0)
    w_dw = w_dw.reshape(width, nc, LANES).transpose(1, 0, 2)
    b_dw = b_dw.reshape(nc, 1, LANES)
    return pl.pallas_call(
        functools.partial(_convmod_kernel, tm=tm, nt=nt),
        grid=(b, nt),
        in_specs=[
            pl.BlockSpec((None, tm, d), lambda i, t: (i, t, 0)),
            pl.BlockSpec((None, HALO, d), prev),
            pl.BlockSpec((None, HALO, d), nxt),
            pl.BlockSpec((None, tm, d), lambda i, t: (i, t, 0)),
            pl.BlockSpec((None, N_MOD, d), lambda i, t: (i, 0, 0)),
            pl.BlockSpec((nc, width, LANES), blocked),
            pl.BlockSpec((nc, 1, LANES), blocked),
            pl.BlockSpec((1, d), row),
            pl.BlockSpec((1, d), row),
            pl.BlockSpec((d, d), row),
            pl.BlockSpec((1, d), row),
        ],
        out_specs=pl.BlockSpec((None, tm, d), lambda i, t: (i, t, 0)),
        out_shape=jax.ShapeDtypeStruct(x.shape, F32),
        scratch_shapes=[pltpu.VMEM((nc, tm + 2 * HALO, LANES), F32), pltpu.VMEM((nc, tm, LANES), F32)],
        compiler_params=_params("parallel", "parallel"),
        name="conv_module_tail",
    )(u, u, u, x, mod, w_dw, b_dw, ln_g, ln_b, w2, b2)


def kernel(x, c, ctx, c_ctx, w_mod, b_mod, pool_w, pool_scale, na_w_qkv, na_w_o, na_rpb, cv_w_pw1, cv_b_pw1, cv_w_dw, cv_b_dw, cv_ln_g, cv_ln_b, cv_w_pw2, cv_b_pw2, ffn_w_up, ffn_w_dw, ffn_b_dw, ffn_w_down, final_norm_g):
    batch, seq, d = x.shape
    depth = w_mod.shape[0]
    lc = ctx.shape[1]
    assert seq % GRID_W == 0 and seq // GRID_W >= WIN_H and d % (2 * LANES) == 0
    tm = min(512, seq)
    tmm = min(1024, seq)
    tmc = lc
    fc = 512

    c_all = jnp.concatenate([c, c_ctx[None]], axis=0)
    m_all = _modulation(c_all, w_mod, b_mod).reshape(depth, 8, N_MOD, d)
    row = lambda v: v.reshape(1, -1)

    for i in range(depth):
        kind, j = i % N_MIXERS, i // N_MIXERS
        update_ctx = i < depth - 1
        mod = m_all[i, :batch]
        mod_c = jnp.broadcast_to(m_all[i, batch:batch + 1], (batch, N_MOD, d))
        if kind == 0:
            w_grp = pool_w[j].astype(BF16)
            x_mid = _pool_mixer(x, mod, w_grp, row(pool_scale[j]), tm)
            if update_ctx:
                ctx_mid = _pool_mixer(ctx, mod_c, w_grp, row(pool_scale[j]), tmc)
        elif kind == 1:
            w_qkv = na_w_qkv[j].astype(BF16)
            w_o = na_w_o[j].astype(BF16)
            qkv = _qkv_proj(x, mod, w_qkv, tmm, 1024)
            qkv_c = _qkv_proj(ctx, mod_c, w_qkv, tmc, 1024)
            attn = _na_attention(qkv, qkv_c, na_rpb[j], NA_ROWS)
            x_mid = _out_proj(attn, x, mod, w_o, tm)
            if update_ctx:
                ctx_mid = _out_proj(_ctx_attention(qkv_c), ctx, mod_c, w_o, tmc)
        else:
            w1 = cv_w_pw1[j].astype(BF16)
            w2 = cv_w_pw2[j].astype(BF16)
            tail = (cv_w_dw[j], row(cv_b_dw[j]), row(cv_ln_g[j]), row(cv_ln_b[j]), w2, row(cv_b_pw2[j]))
            u = _glu_proj(x, mod, w1, row(cv_b_pw1[j]), tmm, 512)
            x_mid = _conv_module_tail(u, x, mod, *tail, tm=256)
            if update_ctx:
                u_c = _glu_proj(ctx, mod_c, w1, row(cv_b_pw1[j]), tmc, 512)
                ctx_mid = _conv_module_tail(u_c, ctx, mod_c, *tail, tm=tmc)
        ffn = (ffn_w_up[i].astype(BF16), ffn_w_dw[i], row(ffn_b_dw[i]), ffn_w_down[i].astype(BF16), row(final_norm_g))
        x = _conv_ffn(x_mid, mod, *ffn, tm=tm, fc=fc, final=(i == depth - 1))
        if update_ctx:
            ctx = _conv_ffn(ctx_mid, mod_c, *ffn, tm=tmc, fc=fc, final=False)
    return x
```

```python
import functools

import numpy as np
import jax
import jax.numpy as jnp
from jax import lax
from jax.experimental import pallas as pl
from jax.experimental.pallas import tpu as pltpu

F32 = jnp.float32
BF16 = jnp.bfloat16

N_MIXERS = 3
N_MOD = 6
POOL_WINDOWS = (2, 4, 8, 16)
GRID_W = 64
NA_HEAD_DIM = 64
WIN_H = 8
WIN_W = 16
NORM_EPS = 1e-6
LN_EPS = 1e-5
NEG_INF = -1e30

LANES = 128
HALO = 16
POOL_SLACK = 16
CONV_NQ = 8
NA_ROWS = 4
VMEM_LIMIT = 58 * 1024 * 1024


def _params(*sem):
    return pltpu.CompilerParams(dimension_semantics=sem, vmem_limit_bytes=VMEM_LIMIT)


def _rms(xf):
    return xf * lax.rsqrt(jnp.mean(xf * xf, axis=-1, keepdims=True) + NORM_EPS)


def _ada(xf, shift, scale):
    return _rms(xf) * (1.0 + scale) + shift


def _halo_maps(tm, seq):
    per = tm // HALO
    last = seq // HALO - 1
    prev = lambda b, t, *_: (b, jnp.maximum(t * per - 1, 0), 0)
    nxt = lambda b, t, *_: (b, jnp.minimum((t + 1) * per, last), 0)
    return prev, nxt


def _mod_kernel(cb_ref, w_ref, b_ref, o_ref):
    w = w_ref[...]
    reps = w.shape[1] // LANES
    n_rows = cb_ref.shape[0]
    for r in range(n_rows):
        cb = jax.nn.silu(cb_ref[r])
        o_ref[r:r + 1, :] = jnp.sum(w * jnp.tile(cb, (1, reps)), axis=0, keepdims=True) + b_ref[...]
    o_ref[n_rows:, :] = jnp.zeros((o_ref.shape[0] - n_rows, w.shape[1]), F32)


def _modulation(c_all, w_mod, b_mod, tn=2048):
    depth, d, n = w_mod.shape
    rows = c_all.shape[0]
    cb = jnp.broadcast_to(c_all[:, :, None], (rows, d, LANES))
    return pl.pallas_call(
        _mod_kernel,
        grid=(depth, n // tn),
        in_specs=[
            pl.BlockSpec((rows, d, LANES), lambda l, j: (0, 0, 0)),
            pl.BlockSpec((None, d, tn), lambda l, j: (l, 0, j)),
            pl.BlockSpec((None, 1, tn), lambda l, j: (l, 0, j)),
        ],
        out_specs=pl.BlockSpec((None, 8, tn), lambda l, j: (l, 0, j)),
        out_shape=jax.ShapeDtypeStruct((depth, 8, n), F32),
        compiler_params=_params("parallel", "parallel"),
        name="modulation",
    )(cb, w_mod, b_mod.reshape(depth, 1, n))


def _pool_kernel(x_ref, xp_ref, xn_ref, mod_ref, w_ref, ls_ref, o_ref, hs_ref, pa_ref, pb_ref, *, tm, nt, seq):
    t = pl.program_id(1)
    shift, scale, gate = mod_ref[0:1, :], mod_ref[1:2, :], mod_ref[2:3, :]
    d = x_ref.shape[1]
    grp = d // len(POOL_WINDOWS)
    body = HALO + tm + HALO
    hs_ref[HALO:HALO + tm, :] = _ada(x_ref[...], shift, scale)
    hs_ref[0:HALO, :] = jnp.where(t > 0, _ada(xp_ref[...], shift, scale), 0.0)
    hs_ref[HALO + tm:body, :] = jnp.where(t < nt - 1, _ada(xn_ref[...], shift, scale), 0.0)
    hs_ref[body:, :] = jnp.zeros((POOL_SLACK, d), F32)
    for p_ref in (pa_ref, pb_ref):
        p_ref[0:8, :] = jnp.zeros((8, grp), F32)
        p_ref[body:, :] = jnp.zeros((POOL_SLACK, grp), F32)
    pos = t * tm + lax.broadcasted_iota(jnp.int32, (tm, grp), 0)
    lo, n = 8, body - 8
    for g, w in enumerate(POOL_WINDOWS):
        cols = slice(g * grp, (g + 1) * grp)
        span, src, src_cols, dst = 1, hs_ref, cols, pa_ref
        while span < w:
            back = w // 2 if 2 * span == w else 0
            dst[lo:lo + n, :] = (src[pl.ds(lo - back, n), src_cols]
                                 + src[pl.ds(lo - back + span, n), src_cols])
            src, src_cols, dst = dst, slice(0, grp), (pb_ref if dst is pa_ref else pa_ref)
            span *= 2
        s = src[HALO:HALO + tm, :]
        cnt = (jnp.clip(pos - w // 2 + w, 0, seq) - jnp.clip(pos - w // 2, 0, seq)).astype(F32)
        dd = (s / cnt - hs_ref[HALO:HALO + tm, cols]).astype(BF16)
        y = jnp.dot(dd, w_ref[g], preferred_element_type=F32)
        o_ref[:, cols] = x_ref[:, cols] + gate[:, cols] * (y * ls_ref[:, cols])


def _pool_mixer(x, mod, w_grp, ls, tm):
    b, seq, d = x.shape
    nt = seq // tm
    grp = d // len(POOL_WINDOWS)
    assert max(POOL_WINDOWS) // 2 <= 8 <= HALO and max(POOL_WINDOWS) // 2 <= POOL_SLACK
    rows = tm + 2 * HALO + POOL_SLACK
    prev, nxt = _halo_maps(tm, seq)
    return pl.pallas_call(
        functools.partial(_pool_kernel, tm=tm, nt=nt, seq=seq),
        grid=(b, nt),
        in_specs=[
            pl.BlockSpec((None, tm, d), lambda i, t: (i, t, 0)),
            pl.BlockSpec((None, HALO, d), prev),
            pl.BlockSpec((None, HALO, d), nxt),
            pl.BlockSpec((None, N_MOD, d), lambda i, t: (i, 0, 0)),
            pl.BlockSpec(w_grp.shape, lambda i, t: (0, 0, 0)),
            pl.BlockSpec((1, d), lambda i, t: (0, 0)),
        ],
        out_specs=pl.BlockSpec((None, tm, d), lambda i, t: (i, t, 0)),
        out_shape=jax.ShapeDtypeStruct(x.shape, F32),
        scratch_shapes=[pltpu.VMEM((rows, d), F32), pltpu.VMEM((rows, grp), F32), pltpu.VMEM((rows, grp), F32)],
        compiler_params=_params("parallel", "parallel"),
        name="pool_mixer",
    )(x, x, x, mod, w_grp, ls)


def _ffn_kernel(x_ref, xp_ref, xn_ref, mod_ref, wg_ref, wv_ref, dwg_ref, dwv_ref, bg_ref, bv_ref,
                wd_ref, fg_ref, o_ref, h_ref, ug_ref, uv_ref, *, tm, nt, final):
    t = pl.program_id(1)
    f = pl.program_id(2)
    shift, scale, gate = mod_ref[3:4, :], mod_ref[4:5, :], mod_ref[5:6, :]

    def conv3(u_ref, dw_ref, b_ref):
        return (dw_ref[0:1, :] * u_ref[pl.ds(HALO - 1, tm), :]
                + dw_ref[1:2, :] * u_ref[pl.ds(HALO, tm), :]
                + dw_ref[2:3, :] * u_ref[pl.ds(HALO + 1, tm), :]
                + b_ref[...])

    def chunk(h, first):
        ug_ref[...] = jnp.dot(h, wg_ref[...], preferred_element_type=F32)
        uv_ref[...] = jnp.dot(h, wv_ref[...], preferred_element_type=F32)
        act = jax.nn.silu(conv3(ug_ref, dwg_ref, bg_ref)) * conv3(uv_ref, dwv_ref, bv_ref)
        part = jnp.dot(act.astype(BF16), wd_ref[...], preferred_element_type=F32)
        if first:
            o_ref[...] = part
        else:
            o_ref[...] += part

    @pl.when(f == 0)
    def _():
        h = jnp.concatenate([
            jnp.where(t > 0, _ada(xp_ref[...], shift, scale), 0.0).astype(BF16),
            _ada(x_ref[...], shift, scale).astype(BF16),
            jnp.where(t < nt - 1, _ada(xn_ref[...], shift, scale), 0.0).astype(BF16)], axis=0)
        h_ref[...] = h
        chunk(h, True)

    @pl.when(f > 0)
    def _():
        chunk(h_ref[...], False)

    @pl.when(f == pl.num_programs(2) - 1)
    def _():
        xn = x_ref[...] + gate * o_ref[...]
        if final:
            xn = _rms(xn) * fg_ref[...]
        o_ref[...] = xn


def _conv_ffn(x, mod, w_up, w_dw, b_dw, w_down, final_g, tm, fc, final):
    b, seq, d = x.shape
    dff = w_down.shape[0]
    nf = dff // fc
    nt = seq // tm
    prev, nxt = _halo_maps(tm, seq)
    x_mode = dict(pipeline_mode=pl.Buffered(1)) if 4 * tm * d * 4 > VMEM_LIMIT // 2 else {}
    return pl.pallas_call(
        functools.partial(_ffn_kernel, tm=tm, nt=nt, final=final),
        grid=(b, nt, nf),
        in_specs=[
            pl.BlockSpec((None, tm, d), lambda i, t, f: (i, t, 0), **x_mode),
            pl.BlockSpec((None, HALO, d), prev),
            pl.BlockSpec((None, HALO, d), nxt),
            pl.BlockSpec((None, N_MOD, d), lambda i, t, f: (i, 0, 0)),
            pl.BlockSpec((d, fc), lambda i, t, f: (0, f)),
            pl.BlockSpec((d, fc), lambda i, t, f: (0, nf + f)),
            pl.BlockSpec((w_dw.shape[0], fc), lambda i, t, f: (0, f)),
            pl.BlockSpec((w_dw.shape[0], fc), lambda i, t, f: (0, nf + f)),
            pl.BlockSpec((1, fc), lambda i, t, f: (0, f)),
            pl.BlockSpec((1, fc), lambda i, t, f: (0, nf + f)),
            pl.BlockSpec((fc, d), lambda i, t, f: (f, 0)),
            pl.BlockSpec((1, d), lambda i, t, f: (0, 0)),
        ],
        out_specs=pl.BlockSpec((None, tm, d), lambda i, t, f: (i, t, 0)),
        out_shape=jax.ShapeDtypeStruct(x.shape, F32),
        scratch_shapes=[
            pltpu.VMEM((tm + 2 * HALO, d), BF16),
            pltpu.VMEM((tm + 2 * HALO, fc), F32),
            pltpu.VMEM((tm + 2 * HALO, fc), F32),
        ],
        compiler_params=_params("parallel", "parallel", "arbitrary"),
        name="conv_ffn",
    )(x, x, x, mod, w_up, w_up, w_dw, w_dw, b_dw, b_dw, w_down, final_g)


def _qkv_kernel(x_ref, mod_ref, w_ref, o_ref, h_ref):
    def project(h):
        o_ref[...] = jnp.dot(h, w_ref[...], preferred_element_type=F32).astype(o_ref.dtype)

    @pl.when(pl.program_id(2) == 0)
    def _():
        h = _ada(x_ref[...], mod_ref[0:1, :], mod_ref[1:2, :]).astype(BF16)
        h_ref[...] = h
        project(h)

    @pl.when(pl.program_id(2) > 0)
    def _():
        project(h_ref[...])


def _qkv_proj(x, mod, w, tm, tn):
    b, seq, d = x.shape
    n = w.shape[1]
    return pl.pallas_call(
        _qkv_kernel,
        grid=(b, seq // tm, n // tn),
        in_specs=[
            pl.BlockSpec((None, tm, d), lambda i, t, j: (i, t, 0)),
            pl.BlockSpec((None, N_MOD, d), lambda i, t, j: (i, 0, 0)),
            pl.BlockSpec((d, tn), lambda i, t, j: (0, j)),
        ],
        out_specs=pl.BlockSpec((None, tm, tn), lambda i, t, j: (i, t, j)),
        out_shape=jax.ShapeDtypeStruct((b, seq, n), BF16),
        scratch_shapes=[pltpu.VMEM((tm, d), BF16)],
        compiler_params=_params("parallel", "parallel", "arbitrary"),
        name="qkv_proj",
    )(x, mod, w)


def _head_pair_queries(q_ref, cols, lo_mask):
    q2 = q_ref[:, cols].astype(F32) * (NA_HEAD_DIM ** -0.5)
    return jnp.concatenate([jnp.where(lo_mask, q2, 0.0), jnp.where(lo_mask, 0.0, q2)], axis=0).astype(BF16)


def _nt_dot(a, b):
    return lax.dot_general(a, b, (((1,), (1,)), ((), ())), preferred_element_type=F32)


def _na_kernel(w0_ref, var_ref, q_ref, k_ref, v_ref, kc_ref, vc_ref, bias_ref, o_ref):
    nq, d = q_ref.shape
    lo_mask = lax.broadcasted_iota(jnp.int32, (nq, LANES), 1) < NA_HEAD_DIM
    ones_loc = jnp.ones((k_ref.shape[1], LANES), BF16)
    ones_ctx = jnp.ones((kc_ref.shape[0], LANES), BF16)
    for p in range(d // LANES):
        cols = slice(p * LANES, (p + 1) * LANES)
        qq = _head_pair_queries(q_ref, cols, lo_mask)
        bias = jnp.concatenate([bias_ref[ri, p, hd] for hd in range(2) for ri in range(bias_ref.shape[0])], axis=0)
        s_loc = _nt_dot(qq, k_ref[0, :, cols]) + bias
        s_ctx = _nt_dot(qq, kc_ref[:, cols])
        m = jnp.maximum(jnp.max(s_loc, axis=-1, keepdims=True), jnp.max(s_ctx, axis=-1, keepdims=True))
        p_loc = jnp.exp(s_loc - m).astype(BF16)
        p_ctx = jnp.exp(s_ctx - m).astype(BF16)
        v2 = jnp.concatenate([v_ref[0, :, cols], ones_loc], axis=1)
        vc2 = jnp.concatenate([vc_ref[:, cols], ones_ctx], axis=1)
        r2 = (jnp.dot(p_loc, v2, preferred_element_type=F32) + jnp.dot(p_ctx, vc2, preferred_element_type=F32))
        r = r2[:, :LANES] / r2[:, LANES:LANES + 1]
        o_ref[:, cols] = jnp.where(lo_mask, r[:nq], r[nq:]).astype(o_ref.dtype)


def _na_plan(rows, rq):
    kr = WIN_H + rq - 1
    keys, w0s, var = [], [], []
    for j in range(rows // rq):
        first = j * rq
        r0 = [min(max(first + ri - WIN_H // 2, 0), rows - WIN_H) for ri in range(rq)]
        w0 = min(r0[0], rows - kr)
        key = (w0 - first,) + tuple(a - w0 for a in r0)
        if key not in keys:
            keys.append(key)
        w0s.append(w0)
        var.append(keys.index(key))
    return keys, np.asarray(w0s, np.int32), np.asarray(var, np.int32)


def _bias_tables(rpb, keys, rq):
    heads = rpb.shape[0]
    kr = WIN_H + rq - 1
    n_off = 2 * WIN_H - 1
    col = jnp.arange(GRID_W)
    col_start = jnp.clip(col - WIN_W // 2, 0, GRID_W - WIN_W)
    valid = (col[None, :] >= col_start[:, None]) & (col[None, :] < col_start[:, None] + WIN_W)
    col_off = jnp.clip(col[None, :] - col[:, None], -(WIN_W - 1), WIN_W - 1) + (WIN_W - 1)
    onehot = (col_off[None] == jnp.arange(2 * WIN_W - 1)[:, None, None]).astype(F32)
    tb = jnp.einsum('hrc,cqk->hqrk', rpb, onehot, precision=lax.Precision.HIGHEST)
    tb = jnp.where(valid[None, :, None, :], tb, NEG_INF).reshape(heads, GRID_W, n_off * GRID_W)
    tb = jnp.pad(tb, ((0, 0), (0, 0), (kr * GRID_W, (kr + 1) * GRID_W)), constant_values=NEG_INF)
    out = []
    for key in keys:
        off, starts = key[0], key[1:]
        per_row = []
        for ri in range(rq):
            krs = np.repeat(np.arange(kr), GRID_W)
            in_win = (krs >= starts[ri]) & (krs < starts[ri] + WIN_H)
            first = (off - ri + WIN_H - 1 + kr) * GRID_W
            per_row.append(jnp.where(jnp.asarray(in_win)[None, None, :], tb[:, :, first:first + kr * GRID_W], NEG_INF))
        out.append(jnp.stack(per_row).reshape(rq, heads // 2, 2, GRID_W, kr * GRID_W))
    return jnp.stack(out)


def _na_attention(qkv, qkv_ctx, rpb, rq):
    b, seq, d3 = qkv.shape
    d = d3 // 3
    rows = seq // GRID_W
    lc = qkv_ctx.shape[1]
    kr = WIN_H + rq - 1
    nq = rq * GRID_W
    keys, w0s, var = _na_plan(rows, rq)
    bias = _bias_tables(rpb, keys, rq)
    win = (pl.Element(1), pl.Element(kr * GRID_W), pl.Element(d))
    grid_spec = pltpu.PrefetchScalarGridSpec(
        num_scalar_prefetch=2,
        grid=(b, rows // rq),
        in_specs=[
            pl.BlockSpec((None, nq, d), lambda i, j, w0, vr: (i, j, 0)),
            pl.BlockSpec(win, lambda i, j, w0, vr: (i, w0[j] * GRID_W, d)),
            pl.BlockSpec(win, lambda i, j, w0, vr: (i, w0[j] * GRID_W, 2 * d)),
            pl.BlockSpec((None, lc, d), lambda i, j, w0, vr: (i, 0, 1)),
            pl.BlockSpec((None, lc, d), lambda i, j, w0, vr: (i, 0, 2)),
            pl.BlockSpec((None,) + bias.shape[1:], lambda i, j, w0, vr: (vr[j], 0, 0, 0, 0, 0),
                         pipeline_mode=pl.Buffered(1)),
        ],
        out_specs=pl.BlockSpec((None, nq, d), lambda i, j, w0, vr: (i, j, 0)),
    )
    return pl.pallas_call(
        _na_kernel,
        grid_spec=grid_spec,
        out_shape=jax.ShapeDtypeStruct((b, seq, d), BF16),
        compiler_params=_params("parallel", "parallel"),
        name="na_attention",
    )(jnp.asarray(w0s), jnp.asarray(var), qkv, qkv, qkv, qkv_ctx, qkv_ctx, bias)


def _ctx_attn_kernel(q_ref, k_ref, v_ref, o_ref):
    nq, d = q_ref.shape
    lo_mask = lax.broadcasted_iota(jnp.int32, (nq, LANES), 1) < NA_HEAD_DIM
    for p in range(d // LANES):
        cols = slice(p * LANES, (p + 1) * LANES)
        qq = _head_pair_queries(q_ref, cols, lo_mask)
        s = _nt_dot(qq, k_ref[:, cols])
        e = jnp.exp(s - jnp.max(s, axis=-1, keepdims=True))
        r = jnp.dot(e.astype(BF16), v_ref[:, cols], preferred_element_type=F32) / jnp.sum(e, axis=-1, keepdims=True)
        o_ref[:, cols] = jnp.where(lo_mask, r[:nq], r[nq:]).astype(o_ref.dtype)


def _ctx_attention(qkv_ctx):
    b, lc, d3 = qkv_ctx.shape
    d = d3 // 3
    return pl.pallas_call(
        _ctx_attn_kernel,
        grid=(b,),
        in_specs=[pl.BlockSpec((None, lc, d), lambda i, j=j: (i, 0, j)) for j in range(3)],
        out_specs=pl.BlockSpec((None, lc, d), lambda i: (i, 0, 0)),
        out_shape=jax.ShapeDtypeStruct((b, lc, d), BF16),
        compiler_params=_params("parallel"),
        name="ctx_attention",
    )(qkv_ctx, qkv_ctx, qkv_ctx)


def _oproj_kernel(a_ref, x_ref, mod_ref, w_ref, o_ref):
    y = jnp.dot(a_ref[...], w_ref[...], preferred_element_type=F32)
    o_ref[...] = x_ref[...] + mod_ref[2:3, :] * y


def _out_proj(a, x, mod, w, tm):
    b, seq, d = x.shape
    return pl.pallas_call(
        _oproj_kernel,
        grid=(b, seq // tm),
        in_specs=[
            pl.BlockSpec((None, tm, d), lambda i, t: (i, t, 0)),
            pl.BlockSpec((None, tm, d), lambda i, t: (i, t, 0)),
            pl.BlockSpec((None, N_MOD, d), lambda i, t: (i, 0, 0)),
            pl.BlockSpec((d, d), lambda i, t: (0, 0)),
        ],
        out_specs=pl.BlockSpec((None, tm, d), lambda i, t: (i, t, 0)),
        out_shape=jax.ShapeDtypeStruct(x.shape, F32),
        compiler_params=_params("parallel", "parallel"),
        name="attn_out_proj",
    )(a, x, mod, w)


def _glu_kernel(x_ref, mod_ref, wa_ref, wg_ref, ba_ref, bg_ref, o_ref, h_ref):
    def project(h):
        a = jnp.dot(h, wa_ref[...], preferred_element_type=F32) + ba_ref[...]
        g = jnp.dot(h, wg_ref[...], preferred_element_type=F32) + bg_ref[...]
        o_ref[...] = a * jax.nn.sigmoid(g)

    @pl.when(pl.program_id(2) == 0)
    def _():
        h = _ada(x_ref[...], mod_ref[0:1, :], mod_ref[1:2, :]).astype(BF16)
        h_ref[...] = h
        project(h)

    @pl.when(pl.program_id(2) > 0)
    def _():
        project(h_ref[...])


def _glu_proj(x, mod, w, bias, tm, tn):
    b, seq, d = x.shape
    nn = d // tn
    return pl.pallas_call(
        _glu_kernel,
        grid=(b, seq // tm, nn),
        in_specs=[
            pl.BlockSpec((None, tm, d), lambda i, t, j: (i, t, 0)),
            pl.BlockSpec((None, N_MOD, d), lambda i, t, j: (i, 0, 0)),
            pl.BlockSpec((d, tn), lambda i, t, j: (0, j)),
            pl.BlockSpec((d, tn), lambda i, t, j: (0, nn + j)),
            pl.BlockSpec((1, tn), lambda i, t, j: (0, j)),
            pl.BlockSpec((1, tn), lambda i, t, j: (0, nn + j)),
        ],
        out_specs=pl.BlockSpec((None, tm, tn), lambda i, t, j: (i, t, j)),
        out_shape=jax.ShapeDtypeStruct((b, seq, d), F32),
        scratch_shapes=[pltpu.VMEM((tm, d), BF16)],
        compiler_params=_params("parallel", "parallel", "arbitrary"),
        name="conv_glu_proj",
    )(x, mod, w, w, bias, bias)


def _convmod_kernel(u_ref, up_ref, un_ref, x_ref, mod_ref, dw_ref, bdw_ref, lng_ref, lnb_ref,
                    w2_ref, b2_ref, o_ref, us_ref, ys_ref, *, tm, nt):
    t = pl.program_id(1)
    nc = us_ref.shape[0]
    width = dw_ref.shape[1]
    half = width // 2
    sub = 8 * CONV_NQ

    def column_block(c, carry):
        cs = pl.ds(pl.multiple_of(c * LANES, LANES), LANES)
        us_ref[c, HALO:HALO + tm, :] = u_ref[:, cs]
        us_ref[c, 0:HALO, :] = jnp.where(t > 0, up_ref[:, cs], 0.0)
        us_ref[c, HALO + tm:, :] = jnp.where(t < nt - 1, un_ref[:, cs], 0.0)
        for m in range(tm // sub):
            e = jnp.concatenate([us_ref[c, pl.ds(HALO + m * sub - half + j, 8, stride=CONV_NQ), :]
                                 for j in range(CONV_NQ + width - 1)], axis=0)
            acc = bdw_ref[c] + dw_ref[c, 0:1, :] * e[0:sub]
            for k in range(1, width):
                acc = acc + dw_ref[c, k:k + 1, :] * e[8 * k:8 * k + sub]
            for q in range(CONV_NQ):
                ys_ref[c, pl.ds(m * sub + q, 8, stride=CONV_NQ), :] = acc[8 * q:8 * q + 8]
        return carry

    lax.fori_loop(0, nc, column_block, 0)
    acc = jnp.concatenate([ys_ref[c] for c in range(nc)], axis=1)
    mu = jnp.mean(acc, axis=-1, keepdims=True)
    cen = acc - mu
    var = jnp.mean(cen * cen, axis=-1, keepdims=True)
    z = jax.nn.silu(cen * lax.rsqrt(var + LN_EPS) * lng_ref[...] + lnb_ref[...])
    y = jnp.dot(z.astype(BF16), w2_ref[...], preferred_element_type=F32) + b2_ref[...]
    o_ref[...] = x_ref[...] + mod_ref[2:3, :] * y


def _conv_module_tail(u, x, mod, w_dw, b_dw, ln_g, ln_b, w2, b2, tm):
    b, seq, d = x.shape
    nt = seq // tm
    nc = d // LANES
    width = w_dw.shape[0]
    assert width // 2 < HALO and tm % (8 * CONV_NQ) == 0
    prev, nxt = _halo_maps(tm, seq)
    row = lambda i, t: (0, 0)
    blocked = lambda i, t: (0, 0, 0)
    w_dw = w_dw.reshape(width, nc, LANES).transpose(1, 0, 2)
    b_dw = b_dw.reshape(nc, 1, LANES)
    return pl.pallas_call(
        functools.partial(_convmod_kernel, tm=tm, nt=nt),
        grid=(b, nt),
        in_specs=[
            pl.BlockSpec((None, tm, d), lambda i, t: (i, t, 0)),
            pl.BlockSpec((None, HALO, d), prev),
            pl.BlockSpec((None, HALO, d), nxt),
            pl.BlockSpec((None, tm, d), lambda i, t: (i, t, 0)),
            pl.BlockSpec((None, N_MOD, d), lambda i, t: (i, 0, 0)),
            pl.BlockSpec((nc, width, LANES), blocked),
            pl.BlockSpec((nc, 1, LANES), blocked),
            pl.BlockSpec((1, d), row),
            pl.BlockSpec((1, d), row),
            pl.BlockSpec((d, d), row),
            pl.BlockSpec((1, d), row),
        ],
        out_specs=pl.BlockSpec((None, tm, d), lambda i, t: (i, t, 0)),
        out_shape=jax.ShapeDtypeStruct(x.shape, F32),
        scratch_shapes=[pltpu.VMEM((nc, tm + 2 * HALO, LANES), F32), pltpu.VMEM((nc, tm, LANES), F32)],
        compiler_params=_params("parallel", "parallel"),
        name="conv_module_tail",
    )(u, u, u, x, mod, w_dw, b_dw, ln_g, ln_b, w2, b2)


def kernel(x, c, ctx, c_ctx, w_mod, b_mod, pool_w, pool_scale, na_w_qkv, na_w_o, na_rpb, cv_w_pw1, cv_b_pw1, cv_w_dw, cv_b_dw, cv_ln_g, cv_ln_b, cv_w_pw2, cv_b_pw2, ffn_w_up, ffn_w_dw, ffn_b_dw, ffn_w_down, final_norm_g):
    batch, seq, d = x.shape
    depth = w_mod.shape[0]
    lc = ctx.shape[1]
    assert seq % GRID_W == 0 and seq // GRID_W >= WIN_H and d % (2 * LANES) == 0
    tm = min(512, seq)
    tmm = min(1024, seq)
    tmc = lc
    fc = 512

    c_all = jnp.concatenate([c, c_ctx[None]], axis=0)
    m_all = _modulation(c_all, w_mod, b_mod).reshape(depth, 8, N_MOD, d)
    row = lambda v: v.reshape(1, -1)

    for i in range(depth):
        kind, j = i % N_MIXERS, i // N_MIXERS
        update_ctx = i < depth - 1
        mod = m_all[i, :batch]
        mod_c = jnp.broadcast_to(m_all[i, batch:batch + 1], (batch, N_MOD, d))
        if kind == 0:
            w_grp = pool_w[j].astype(BF16)
            x_mid = _pool_mixer(x, mod, w_grp, row(pool_scale[j]), tm)
            if update_ctx:
                ctx_mid = _pool_mixer(ctx, mod_c, w_grp, row(pool_scale[j]), tmc)
        elif kind == 1:
            w_qkv = na_w_qkv[j].astype(BF16)
            w_o = na_w_o[j].astype(BF16)
            qkv = _qkv_proj(x, mod, w_qkv, tmm, 1024)
            qkv_c = _qkv_proj(ctx, mod_c, w_qkv, tmc, 1024)
            attn = _na_attention(qkv, qkv_c, na_rpb[j], NA_ROWS)
            x_mid = _out_proj(attn, x, mod, w_o, tm)
            if update_ctx:
                ctx_mid = _out_proj(_ctx_attention(qkv_c), ctx, mod_c, w_o, tmc)
        else:
            w1 = cv_w_pw1[j].astype(BF16)
            w2 = cv_w_pw2[j].astype(BF16)
            tail = (cv_w_dw[j], row(cv_b_dw[j]), row(cv_ln_g[j]), row(cv_ln_b[j]), w2, row(cv_b_pw2[j]))
            u = _glu_proj(x, mod, w1, row(cv_b_pw1[j]), tmm, 512)
            x_mid = _conv_module_tail(u, x, mod, *tail, tm=256)
            if update_ctx:
                u_c = _glu_proj(ctx, mod_c, w1, row(cv_b_pw1[j]), tmc, 512)
                ctx_mid = _conv_module_tail(u_c, ctx, mod_c, *tail, tm=tmc)
        ffn = (ffn_w_up[i].astype(BF16), ffn_w_dw[i], row(ffn_b_dw[i]), ffn_w_down[i].astype(BF16), row(final_norm_g))
        x = _conv_ffn(x_mid, mod, *ffn, tm=tm, fc=fc, final=(i == depth - 1))
        if update_ctx:
            ctx = _conv_ffn(ctx_mid, mod_c, *ffn, tm=tmc, fc=fc, final=False)
    return x
```

```python
import functools

import numpy as np
import jax
import jax.numpy as jnp
from jax import lax
from jax.experimental import pallas as pl
from jax.experimental.pallas import tpu as pltpu

F32 = jnp.float32
BF16 = jnp.bfloat16

N_MIXERS = 3
N_MOD = 6
POOL_WINDOWS = (2, 4, 8, 16)
GRID_W = 64
NA_HEAD_DIM = 64
WIN_H = 8
WIN_W = 16
NORM_EPS = 1e-6
LN_EPS = 1e-5
NEG_INF = -1e30

LANES = 128
HALO = 16
POOL_SLACK = 16
CONV_NQ = 8
NA_ROWS = 4
VMEM_LIMIT = 58 * 1024 * 1024


def _params(*sem):
    return pltpu.CompilerParams(dimension_semantics=sem, vmem_limit_bytes=VMEM_LIMIT)


def _rms(xf):
    return xf * lax.rsqrt(jnp.mean(xf * xf, axis=-1, keepdims=True) + NORM_EPS)


def _ada(xf, shift, scale):
    return _rms(xf) * (1.0 + scale) + shift


def _halo_maps(tm, seq):
    per = tm // HALO
    last = seq // HALO - 1
    prev = lambda b, t, *_: (b, jnp.maximum(t * per - 1, 0), 0)
    nxt = lambda b, t, *_: (b, jnp.minimum((t + 1) * per, last), 0)
    return prev, nxt


def _mod_kernel(cb_ref, w_ref, b_ref, o_ref):
    w = w_ref[...]
    reps = w.shape[1] // LANES
    n_rows = cb_ref.shape[0]
    for r in range(n_rows):
        cb = jax.nn.silu(cb_ref[r])
        o_ref[r:r + 1, :] = jnp.sum(w * jnp.tile(cb, (1, reps)), axis=0, keepdims=True) + b_ref[...]
    o_ref[n_rows:, :] = jnp.zeros((o_ref.shape[0] - n_rows, w.shape[1]), F32)


def _modulation(c_all, w_mod, b_mod, tn=2048):
    depth, d, n = w_mod.shape
    rows = c_all.shape[0]
    cb = jnp.broadcast_to(c_all[:, :, None], (rows, d, LANES))
    return pl.pallas_call(
        _mod_kernel,
        grid=(depth, n // tn),
        in_specs=[
            pl.BlockSpec((rows, d, LANES), lambda l, j: (0, 0, 0)),
            pl.BlockSpec((None, d, tn), lambda l, j: (l, 0, j)),
            pl.BlockSpec((None, 1, tn), lambda l, j: (l, 0, j)),
        ],
        out_specs=pl.BlockSpec((None, 8, tn), lambda l, j: (l, 0, j)),
        out_shape=jax.ShapeDtypeStruct((depth, 8, n), F32),
        compiler_params=_params("parallel", "parallel"),
        name="modulation",
    )(cb, w_mod, b_mod.reshape(depth, 1, n))


def _pool_kernel(x_ref, xp_ref, xn_ref, mod_ref, w_ref, ls_ref, o_ref, hs_ref, pa_ref, pb_ref, *, tm, nt, seq):
    t = pl.program_id(1)
    shift, scale, gate = mod_ref[0:1, :], mod_ref[1:2, :], mod_ref[2:3, :]
    d = x_ref.shape[1]
    grp = d // len(POOL_WINDOWS)
    body = HALO + tm + HALO
    hs_ref[HALO:HALO + tm, :] = _ada(x_ref[...], shift, scale)
    hs_ref[0:HALO, :] = jnp.where(t > 0, _ada(xp_ref[...], shift, scale), 0.0)
    hs_ref[HALO + tm:body, :] = jnp.where(t < nt - 1, _ada(xn_ref[...], shift, scale), 0.0)
    hs_ref[body:, :] = jnp.zeros((POOL_SLACK, d), F32)
    for p_ref in (pa_ref, pb_ref):
        p_ref[0:8, :] = jnp.zeros((8, grp), F32)
        p_ref[body:, :] = jnp.zeros((POOL_SLACK, grp), F32)
    pos = t * tm + lax.broadcasted_iota(jnp.int32, (tm, grp), 0)
    lo, n = 8, body - 8
    for g, w in enumerate(POOL_WINDOWS):
        cols = slice(g * grp, (g + 1) * grp)
        span, src, src_cols, dst = 1, hs_ref, cols, pa_ref
        while span < w:
            back = w // 2 if 2 * span == w else 0
            dst[lo:lo + n, :] = (src[pl.ds(lo - back, n), src_cols]
                                 + src[pl.ds(lo - back + span, n), src_cols])
            src, src_cols, dst = dst, slice(0, grp), (pb_ref if dst is pa_ref else pa_ref)
            span *= 2
        s = src[HALO:HALO + tm, :]
        cnt = (jnp.clip(pos - w // 2 + w, 0, seq) - jnp.clip(pos - w // 2, 0, seq)).astype(F32)
        dd = (s / cnt - hs_ref[HALO:HALO + tm, cols]).astype(BF16)
        y = jnp.dot(dd, w_ref[g], preferred_element_type=F32)
        o_ref[:, cols] = x_ref[:, cols] + gate[:, cols] * (y * ls_ref[:, cols])


def _pool_mixer(x, mod, w_grp, ls, tm):
    b, seq, d = x.shape
    nt = seq // tm
    grp = d // len(POOL_WINDOWS)
    assert max(POOL_WINDOWS) // 2 <= 8 <= HALO and max(POOL_WINDOWS) // 2 <= POOL_SLACK
    rows = tm + 2 * HALO + POOL_SLACK
    prev, nxt = _halo_maps(tm, seq)
    return pl.pallas_call(
        functools.partial(_pool_kernel, tm=tm, nt=nt, seq=seq),
        grid=(b, nt),
        in_specs=[
            pl.BlockSpec((None, tm, d), lambda i, t: (i, t, 0)),
            pl.BlockSpec((None, HALO, d), prev),
            pl.BlockSpec((None, HALO, d), nxt),
            pl.BlockSpec((None, N_MOD, d), lambda i, t: (i, 0, 0)),
            pl.BlockSpec(w_grp.shape, lambda i, t: (0, 0, 0)),
            pl.BlockSpec((1, d), lambda i, t: (0, 0)),
        ],
        out_specs=pl.BlockSpec((None, tm, d), lambda i, t: (i, t, 0)),
        out_shape=jax.ShapeDtypeStruct(x.shape, F32),
        scratch_shapes=[pltpu.VMEM((rows, d), F32), pltpu.VMEM((rows, grp), F32), pltpu.VMEM((rows, grp), F32)],
        compiler_params=_params("parallel", "parallel"),
        name="pool_mixer",
    )(x, x, x, mod, w_grp, ls)


def _ffn_kernel(x_ref, xp_ref, xn_ref, mod_ref, wg_ref, wv_ref, dwg_ref, dwv_ref, bg_ref, bv_ref,
                wd_ref, fg_ref, o_ref, h_ref, ug_ref, uv_ref, *, tm, nt, final):
    t = pl.program_id(1)
    f = pl.program_id(2)
    shift, scale, gate = mod_ref[3:4, :], mod_ref[4:5, :], mod_ref[5:6, :]

    def conv3(u_ref, dw_ref, b_ref):
        return (dw_ref[0:1, :] * u_ref[pl.ds(HALO - 1, tm), :]
                + dw_ref[1:2, :] * u_ref[pl.ds(HALO, tm), :]
                + dw_ref[2:3, :] * u_ref[pl.ds(HALO + 1, tm), :]
                + b_ref[...])

    def chunk(h, first):
        ug_ref[...] = jnp.dot(h, wg_ref[...], preferred_element_type=F32)
        uv_ref[...] = jnp.dot(h, wv_ref[...], preferred_element_type=F32)
        act = jax.nn.silu(conv3(ug_ref, dwg_ref, bg_ref)) * conv3(uv_ref, dwv_ref, bv_ref)
        part = jnp.dot(act.astype(BF16), wd_ref[...], preferred_element_type=F32)
        if first:
            o_ref[...] = part
        else:
            o_ref[...] += part

    @pl.when(f == 0)
    def _():
        h = jnp.concatenate([
            jnp.where(t > 0, _ada(xp_ref[...], shift, scale), 0.0).astype(BF16),
            _ada(x_ref[...], shift, scale).astype(BF16),
            jnp.where(t < nt - 1, _ada(xn_ref[...], shift, scale), 0.0).astype(BF16)], axis=0)
        h_ref[...] = h
        chunk(h, True)

    last = pl.num_programs(2) - 1

    @pl.when((f > 0) & (f < last))
    def _():
        chunk(h_ref[...], False)

    @pl.when(f == last)
    def _():
        chunk(h_ref[...], False)
        xn = x_ref[...] + gate * o_ref[...]
        if final:
            xn = _rms(xn) * fg_ref[...]
        o_ref[...] = xn


def _conv_ffn(x, mod, w_up, w_dw, b_dw, w_down, final_g, tm, fc, final):
    b, seq, d = x.shape
    dff = w_down.shape[0]
    nf = dff // fc
    nt = seq // tm
    assert nf >= 2
    prev, nxt = _halo_maps(tm, seq)
    x_mode = dict(pipeline_mode=pl.Buffered(1)) if 4 * tm * d * 4 > VMEM_LIMIT // 2 else {}
    return pl.pallas_call(
        functools.partial(_ffn_kernel, tm=tm, nt=nt, final=final),
        grid=(b, nt, nf),
        in_specs=[
            pl.BlockSpec((None, tm, d), lambda i, t, f: (i, t, 0), **x_mode),
            pl.BlockSpec((None, HALO, d), prev),
            pl.BlockSpec((None, HALO, d), nxt),
            pl.BlockSpec((None, N_MOD, d), lambda i, t, f: (i, 0, 0)),
            pl.BlockSpec((d, fc), lambda i, t, f: (0, f)),
            pl.BlockSpec((d, fc), lambda i, t, f: (0, nf + f)),
            pl.BlockSpec((w_dw.shape[0], fc), lambda i, t, f: (0, f)),
            pl.BlockSpec((w_dw.shape[0], fc), lambda i, t, f: (0, nf + f)),
            pl.BlockSpec((1, fc), lambda i, t, f: (0, f)),
            pl.BlockSpec((1, fc), lambda i, t, f: (0, nf + f)),
            pl.BlockSpec((fc, d), lambda i, t, f: (f, 0)),
            pl.BlockSpec((1, d), lambda i, t, f: (0, 0)),
        ],
        out_specs=pl.BlockSpec((None, tm, d), lambda i, t, f: (i, t, 0)),
        out_shape=jax.ShapeDtypeStruct(x.shape, F32),
        scratch_shapes=[
            pltpu.VMEM((tm + 2 * HALO, d), BF16),
            pltpu.VMEM((tm + 2 * HALO, fc), F32),
            pltpu.VMEM((tm + 2 * HALO, fc), F32),
        ],
        compiler_params=_params("parallel", "parallel", "arbitrary"),
        name="conv_ffn",
    )(x, x, x, mod, w_up, w_up, w_dw, w_dw, b_dw, b_dw, w_down, final_g)


def _qkv_kernel(x_ref, mod_ref, w_ref, o_ref, h_ref):
    def project(h):
        o_ref[...] = jnp.dot(h, w_ref[...], preferred_element_type=F32).astype(o_ref.dtype)

    @pl.when(pl.program_id(2) == 0)
    def _():
        h = _ada(x_ref[...], mod_ref[0:1, :], mod_ref[1:2, :]).astype(BF16)
        h_ref[...] = h
        project(h)

    @pl.when(pl.program_id(2) > 0)
    def _():
        project(h_ref[...])


def _qkv_proj(x, mod, w, tm, tn):
    b, seq, d = x.shape
    n = w.shape[1]
    return pl.pallas_call(
        _qkv_kernel,
        grid=(b, seq // tm, n // tn),
        in_specs=[
            pl.BlockSpec((None, tm, d), lambda i, t, j: (i, t, 0)),
            pl.BlockSpec((None, N_MOD, d), lambda i, t, j: (i, 0, 0)),
            pl.BlockSpec((d, tn), lambda i, t, j: (0, j)),
        ],
        out_specs=pl.BlockSpec((None, tm, tn), lambda i, t, j: (i, t, j)),
        out_shape=jax.ShapeDtypeStruct((b, seq, n), BF16),
        scratch_shapes=[pltpu.VMEM((tm, d), BF16)],
        compiler_params=_params("parallel", "parallel", "arbitrary"),
        name="qkv_proj",
    )(x, mod, w)


def _head_pair_queries(q_ref, cols, lo_mask):
    q2 = q_ref[:, cols].astype(F32) * (NA_HEAD_DIM ** -0.5)
    return jnp.concatenate([jnp.where(lo_mask, q2, 0.0), jnp.where(lo_mask, 0.0, q2)], axis=0).astype(BF16)


def _nt_dot(a, b):
    return lax.dot_general(a, b, (((1,), (1,)), ((), ())), preferred_element_type=F32)


def _na_kernel(w0_ref, var_ref, q_ref, k_ref, v_ref, kc_ref, vc_ref, bias_ref, o_ref):
    nq, d = q_ref.shape
    lo_mask = lax.broadcasted_iota(jnp.int32, (nq, LANES), 1) < NA_HEAD_DIM
    ones_loc = jnp.ones((k_ref.shape[1], LANES), BF16)
    ones_ctx = jnp.ones((kc_ref.shape[0], LANES), BF16)
    for p in range(d // LANES):
        cols = slice(p * LANES, (p + 1) * LANES)
        qq = _head_pair_queries(q_ref, cols, lo_mask)
        bias = jnp.concatenate([bias_ref[ri, p, hd] for hd in range(2) for ri in range(bias_ref.shape[0])], axis=0)
        s_loc = _nt_dot(qq, k_ref[0, :, cols]) + bias
        s_ctx = _nt_dot(qq, kc_ref[:, cols])
        m = jnp.maximum(jnp.max(s_loc, axis=-1, keepdims=True), jnp.max(s_ctx, axis=-1, keepdims=True))
        p_loc = jnp.exp(s_loc - m).astype(BF16)
        p_ctx = jnp.exp(s_ctx - m).astype(BF16)
        v2 = jnp.concatenate([v_ref[0, :, cols], ones_loc], axis=1)
        vc2 = jnp.concatenate([vc_ref[:, cols], ones_ctx], axis=1)
        r2 = (jnp.dot(p_loc, v2, preferred_element_type=F32) + jnp.dot(p_ctx, vc2, preferred_element_type=F32))
        r = r2[:, :LANES] / r2[:, LANES:LANES + 1]
        o_ref[:, cols] = jnp.where(lo_mask, r[:nq], r[nq:]).astype(o_ref.dtype)


def _na_plan(rows, rq):
    kr = WIN_H + rq - 1
    keys, w0s, var = [], [], []
    for j in range(rows // rq):
        first = j * rq
        r0 = [min(max(first + ri - WIN_H // 2, 0), rows - WIN_H) for ri in range(rq)]
        w0 = min(r0[0], rows - kr)
        key = (w0 - first,) + tuple(a - w0 for a in r0)
        if key not in keys:
            keys.append(key)
        w0s.append(w0)
        var.append(keys.index(key))
    return keys, np.asarray(w0s, np.int32), np.asarray(var, np.int32)


def _bias_tables(rpb, keys, rq):
    heads = rpb.shape[0]
    kr = WIN_H + rq - 1
    n_off = 2 * WIN_H - 1
    col = jnp.arange(GRID_W)
    col_start = jnp.clip(col - WIN_W // 2, 0, GRID_W - WIN_W)
    valid = (col[None, :] >= col_start[:, None]) & (col[None, :] < col_start[:, None] + WIN_W)
    col_off = jnp.clip(col[None, :] - col[:, None], -(WIN_W - 1), WIN_W - 1) + (WIN_W - 1)
    onehot = (col_off[None] == jnp.arange(2 * WIN_W - 1)[:, None, None]).astype(F32)
    tb = jnp.einsum('hrc,cqk->hqrk', rpb, onehot, precision=lax.Precision.HIGHEST)
    tb = jnp.where(valid[None, :, None, :], tb, NEG_INF).reshape(heads, GRID_W, n_off * GRID_W)
    tb = jnp.pad(tb, ((0, 0), (0, 0), (kr * GRID_W, (kr + 1) * GRID_W)), constant_values=NEG_INF)
    out = []
    for key in keys:
        off, starts = key[0], key[1:]
        per_row = []
        for ri in range(rq):
            krs = np.repeat(np.arange(kr), GRID_W)
            in_win = (krs >= starts[ri]) & (krs < starts[ri] + WIN_H)
            first = (off - ri + WIN_H - 1 + kr) * GRID_W
            per_row.append(jnp.where(jnp.asarray(in_win)[None, None, :], tb[:, :, first:first + kr * GRID_W], NEG_INF))
        out.append(jnp.stack(per_row).reshape(rq, heads // 2, 2, GRID_W, kr * GRID_W))
    return jnp.stack(out)


def _na_attention(qkv, qkv_ctx, rpb, rq):
    b, seq, d3 = qkv.shape
    d = d3 // 3
    rows = seq // GRID_W
    lc = qkv_ctx.shape[1]
    kr = WIN_H + rq - 1
    nq = rq * GRID_W
    keys, w0s, var = _na_plan(rows, rq)
    bias = _bias_tables(rpb, keys, rq)
    win = (pl.Element(1), pl.Element(kr * GRID_W), pl.Element(d))
    grid_spec = pltpu.PrefetchScalarGridSpec(
        num_scalar_prefetch=2,
        grid=(b, rows // rq),
        in_specs=[
            pl.BlockSpec((None, nq, d), lambda i, j, w0, vr: (i, j, 0)),
            pl.BlockSpec(win, lambda i, j, w0, vr: (i, w0[j] * GRID_W, d)),
            pl.BlockSpec(win, lambda i, j, w0, vr: (i, w0[j] * GRID_W, 2 * d)),
            pl.BlockSpec((None, lc, d), lambda i, j, w0, vr: (i, 0, 1)),
            pl.BlockSpec((None, lc, d), lambda i, j, w0, vr: (i, 0, 2)),
            pl.BlockSpec((None,) + bias.shape[1:], lambda i, j, w0, vr: (vr[j], 0, 0, 0, 0, 0),
                         pipeline_mode=pl.Buffered(1)),
        ],
        out_specs=pl.BlockSpec((None, nq, d), lambda i, j, w0, vr: (i, j, 0)),
    )
    return pl.pallas_call(
        _na_kernel,
        grid_spec=grid_spec,
        out_shape=jax.ShapeDtypeStruct((b, seq, d), BF16),
        compiler_params=_params("parallel", "parallel"),
        name="na_attention",
    )(jnp.asarray(w0s), jnp.asarray(var), qkv, qkv, qkv, qkv_ctx, qkv_ctx, bias)


def _ctx_attn_kernel(q_ref, k_ref, v_ref, o_ref):
    nq, d = q_ref.shape
    lo_mask = lax.broadcasted_iota(jnp.int32, (nq, LANES), 1) < NA_HEAD_DIM
    for p in range(d // LANES):
        cols = slice(p * LANES, (p + 1) * LANES)
        qq = _head_pair_queries(q_ref, cols, lo_mask)
        s = _nt_dot(qq, k_ref[:, cols])
        e = jnp.exp(s - jnp.max(s, axis=-1, keepdims=True))
        r = jnp.dot(e.astype(BF16), v_ref[:, cols], preferred_element_type=F32) / jnp.sum(e, axis=-1, keepdims=True)
        o_ref[:, cols] = jnp.where(lo_mask, r[:nq], r[nq:]).astype(o_ref.dtype)


def _ctx_attention(qkv_ctx):
    b, lc, d3 = qkv_ctx.shape
    d = d3 // 3
    return pl.pallas_call(
        _ctx_attn_kernel,
        grid=(b,),
        in_specs=[pl.BlockSpec((None, lc, d), lambda i, j=j: (i, 0, j)) for j in range(3)],
        out_specs=pl.BlockSpec((None, lc, d), lambda i: (i, 0, 0)),
        out_shape=jax.ShapeDtypeStruct((b, lc, d), BF16),
        compiler_params=_params("parallel"),
        name="ctx_attention",
    )(qkv_ctx, qkv_ctx, qkv_ctx)


def _oproj_kernel(a_ref, x_ref, mod_ref, w_ref, o_ref):
    y = jnp.dot(a_ref[...], w_ref[...], preferred_element_type=F32)
    o_ref[...] = x_ref[...] + mod_ref[2:3, :] * y


def _out_proj(a, x, mod, w, tm):
    b, seq, d = x.shape
    return pl.pallas_call(
        _oproj_kernel,
        grid=(b, seq // tm),
        in_specs=[
            pl.BlockSpec((None, tm, d), lambda i, t: (i, t, 0)),
            pl.BlockSpec((None, tm, d), lambda i, t: (i, t, 0)),
            pl.BlockSpec((None, N_MOD, d), lambda i, t: (i, 0, 0)),
            pl.BlockSpec((d, d), lambda i, t: (0, 0)),
        ],
        out_specs=pl.BlockSpec((None, tm, d), lambda i, t: (i, t, 0)),
        out_shape=jax.ShapeDtypeStruct(x.shape, F32),
        compiler_params=_params("parallel", "parallel"),
        name="attn_out_proj",
    )(a, x, mod, w)


def _glu_kernel(x_ref, mod_ref, wa_ref, wg_ref, ba_ref, bg_ref, o_ref, h_ref):
    def project(h):
        a = jnp.dot(h, wa_ref[...], preferred_element_type=F32) + ba_ref[...]
        g = jnp.dot(h, wg_ref[...], preferred_element_type=F32) + bg_ref[...]
        o_ref[...] = a * jax.nn.sigmoid(g)

    @pl.when(pl.program_id(2) == 0)
    def _():
        h = _ada(x_ref[...], mod_ref[0:1, :], mod_ref[1:2, :]).astype(BF16)
        h_ref[...] = h
        project(h)

    @pl.when(pl.program_id(2) > 0)
    def _():
        project(h_ref[...])


def _glu_proj(x, mod, w, bias, tm, tn):
    b, seq, d = x.shape
    nn = d // tn
    return pl.pallas_call(
        _glu_kernel,
        grid=(b, seq // tm, nn),
        in_specs=[
            pl.BlockSpec((None, tm, d), lambda i, t, j: (i, t, 0)),
            pl.BlockSpec((None, N_MOD, d), lambda i, t, j: (i, 0, 0)),
            pl.BlockSpec((d, tn), lambda i, t, j: (0, j)),
            pl.BlockSpec((d, tn), lambda i, t, j: (0, nn + j)),
            pl.BlockSpec((1, tn), lambda i, t, j: (0, j)),
            pl.BlockSpec((1, tn), lambda i, t, j: (0, nn + j)),
        ],
        out_specs=pl.BlockSpec((None, tm, tn), lambda i, t, j: (i, t, j)),
        out_shape=jax.ShapeDtypeStruct((b, seq, d), F32),
        scratch_shapes=[pltpu.VMEM((tm, d), BF16)],
        compiler_params=_params("parallel", "parallel", "arbitrary"),
        name="conv_glu_proj",
    )(x, mod, w, w, bias, bias)


def _convmod_kernel(u_ref, up_ref, un_ref, x_ref, mod_ref, dw_ref, bdw_ref, lng_ref, lnb_ref,
                    w2_ref, b2_ref, o_ref, us_ref, ys_ref, *, tm, nt):
    t = pl.program_id(1)
    nc = us_ref.shape[0]
    width = dw_ref.shape[1]
    half = width // 2
    sub = 8 * CONV_NQ

    def column_block(c, carry):
        cs = pl.ds(pl.multiple_of(c * LANES, LANES), LANES)
        us_ref[c, HALO:HALO + tm, :] = u_ref[:, cs]
        us_ref[c, 0:HALO, :] = jnp.where(t > 0, up_ref[:, cs], 0.0)
        us_ref[c, HALO + tm:, :] = jnp.where(t < nt - 1, un_ref[:, cs], 0.0)
        for m in range(tm // sub):
            e = jnp.concatenate([us_ref[c, pl.ds(HALO + m * sub - half + j, 8, stride=CONV_NQ), :]
                                 for j in range(CONV_NQ + width - 1)], axis=0)
            acc = bdw_ref[c] + dw_ref[c, 0:1, :] * e[0:sub]
            for k in range(1, width):
                acc = acc + dw_ref[c, k:k + 1, :] * e[8 * k:8 * k + sub]
            for q in range(CONV_NQ):
                ys_ref[c, pl.ds(m * sub + q, 8, stride=CONV_NQ), :] = acc[8 * q:8 * q + 8]
        return carry

    lax.fori_loop(0, nc, column_block, 0)
    acc = jnp.concatenate([ys_ref[c] for c in range(nc)], axis=1)
    mu = jnp.mean(acc, axis=-1, keepdims=True)
    cen = acc - mu
    var = jnp.mean(cen * cen, axis=-1, keepdims=True)
    z = jax.nn.silu(cen * lax.rsqrt(var + LN_EPS) * lng_ref[...] + lnb_ref[...])
    y = jnp.dot(z.astype(BF16), w2_ref[...], preferred_element_type=F32) + b2_ref[...]
    o_ref[...] = x_ref[...] + mod_ref[2:3, :] * y


def _conv_module_tail(u, x, mod, w_dw, b_dw, ln_g, ln_b, w2, b2, tm):
    b, seq, d = x.shape
    nt = seq // tm
    nc = d // LANES
    width = w_dw.shape[0]
    assert width // 2 < HALO and tm % (8 * CONV_NQ) == 0
    prev, nxt = _halo_maps(tm, seq)
    row = lambda i, t: (0, 0)
    blocked = lambda i, t: (0, 0, 0)
    w_dw = w_dw.reshape(width, nc, LANES).transpose(1, 0, 2)
    b_dw = b_dw.reshape(nc, 1, LANES)
    return pl.pallas_call(
        functools.partial(_convmod_kernel, tm=tm, nt=nt),
        grid=(b, nt),
        in_specs=[
            pl.BlockSpec((None, tm, d), lambda i, t: (i, t, 0)),
            pl.BlockSpec((None, HALO, d), prev),
            pl.BlockSpec((None, HALO, d), nxt),
            pl.BlockSpec((None, tm, d), lambda i, t: (i, t, 0)),
            pl.BlockSpec((None, N_MOD, d), lambda i, t: (i, 0, 0)),
            pl.BlockSpec((nc, width, LANES), blocked),
            pl.BlockSpec((nc, 1, LANES), blocked),
            pl.BlockSpec((1, d), row),
            pl.BlockSpec((1, d), row),
            pl.BlockSpec((d, d), row),
            pl.BlockSpec((1, d), row),
        ],
        out_specs=pl.BlockSpec((None, tm, d), lambda i, t: (i, t, 0)),
        out_shape=jax.ShapeDtypeStruct(x.shape, F32),
        scratch_shapes=[pltpu.VMEM((nc, tm + 2 * HALO, LANES), F32), pltpu.VMEM((nc, tm, LANES), F32)],
        compiler_params=_params("parallel", "parallel"),
        name="conv_module_tail",
    )(u, u, u, x, mod, w_dw, b_dw, ln_g, ln_b, w2, b2)


def kernel(x, c, ctx, c_ctx, w_mod, b_mod, pool_w, pool_scale, na_w_qkv, na_w_o, na_rpb, cv_w_pw1, cv_b_pw1, cv_w_dw, cv_b_dw, cv_ln_g, cv_ln_b, cv_w_pw2, cv_b_pw2, ffn_w_up, ffn_w_dw, ffn_b_dw, ffn_w_down, final_norm_g):
    batch, seq, d = x.shape
    depth = w_mod.shape[0]
    lc = ctx.shape[1]
    assert seq % GRID_W == 0 and seq // GRID_W >= WIN_H and d % (2 * LANES) == 0
    tm = min(512, seq)
    tmm = min(1024, seq)
    tmc = lc
    fc = 512

    c_all = jnp.concatenate([c, c_ctx[None]], axis=0)
    m_all = _modulation(c_all, w_mod, b_mod).reshape(depth, 8, N_MOD, d)
    row = lambda v: v.reshape(1, -1)

    for i in range(depth):
        kind, j = i % N_MIXERS, i // N_MIXERS
        update_ctx = i < depth - 1
        mod = m_all[i, :batch]
        mod_c = jnp.broadcast_to(m_all[i, batch:batch + 1], (batch, N_MOD, d))
        if kind == 0:
            w_grp = pool_w[j].astype(BF16)
            x_mid = _pool_mixer(x, mod, w_grp, row(pool_scale[j]), tm)
            if update_ctx:
                ctx_mid = _pool_mixer(ctx, mod_c, w_grp, row(pool_scale[j]), tmc)
        elif kind == 1:
            w_qkv = na_w_qkv[j].astype(BF16)
            w_o = na_w_o[j].astype(BF16)
            qkv = _qkv_proj(x, mod, w_qkv, tmm, 1024)
            qkv_c = _qkv_proj(ctx, mod_c, w_qkv, tmc, 1024)
            attn = _na_attention(qkv, qkv_c, na_rpb[j], NA_ROWS)
            x_mid = _out_proj(attn, x, mod, w_o, tm)
            if update_ctx:
                ctx_mid = _out_proj(_ctx_attention(qkv_c), ctx, mod_c, w_o, tmc)
        else:
            w1 = cv_w_pw1[j].astype(BF16)
            w2 = cv_w_pw2[j].astype(BF16)
            tail = (cv_w_dw[j], row(cv_b_dw[j]), row(cv_ln_g[j]), row(cv_ln_b[j]), w2, row(cv_b_pw2[j]))
            u = _glu_proj(x, mod, w1, row(cv_b_pw1[j]), tmm, 512)
            x_mid = _conv_module_tail(u, x, mod, *tail, tm=256)
            if update_ctx:
                u_c = _glu_proj(ctx, mod_c, w1, row(cv_b_pw1[j]), tmc, 512)
                ctx_mid = _conv_module_tail(u_c, ctx, mod_c, *tail, tm=tmc)
        ffn = (ffn_w_up[i].astype(BF16), ffn_w_dw[i], row(ffn_b_dw[i]), ffn_w_down[i].astype(BF16), row(final_norm_g))
        x = _conv_ffn(x_mid, mod, *ffn, tm=tm, fc=fc, final=(i == depth - 1))
        if update_ctx:
            ctx = _conv_ffn(ctx_mid, mod_c, *ffn, tm=tmc, fc=fc, final=False)
    return x
```
